```python
import math
import jax, jax.numpy as jnp
from jax import lax
import numpy as np

D_MODEL = 1024
BATCH = 8
SEQ = 4096
DEPTH = 1

CHUNK = 64
SB_HEADS = 8
SB_HEAD_DIM = D_MODEL // 16
SB_WIDTH = SB_HEADS * SB_HEAD_DIM
CA_HEADS = 8
CA_HEAD_DIM = D_MODEL // 16
CA_WIDTH = CA_HEADS * CA_HEAD_DIM
CA_PREV_CHUNKS = 8
CA_BAND = (CA_PREV_CHUNKS + 1) * CHUNK
REL_CLIP = 256
Q_BLOCK = 128
D_FF = 4 * D_MODEL
DEEPNORM_ALPHA = (2.0 * DEPTH) ** 0.25
DEEPNORM_BETA = (8.0 * DEPTH) ** -0.25
LN_EPS = 1e-5
IN_COLS = 3 * SB_WIDTH + 3 * CA_WIDTH + 2 * D_MODEL

kernel_name = "hybrid_stickbreak_chunkrel_deepnorm"


def _layer_norm(x, g, b):
    xf = x.astype(jnp.float32)
    mu = jnp.mean(xf, axis=-1, keepdims=True)
    var = jnp.mean(jnp.square(xf - mu), axis=-1, keepdims=True)
    y = (xf - mu) * lax.rsqrt(var + LN_EPS) * g.astype(jnp.float32) + b.astype(jnp.float32)
    return y.astype(x.dtype)


def _stick_breaking(q, k, v):
    b, s_len, h, dh = q.shape
    scale = dh ** -0.5
    qh = q.transpose(0, 2, 1, 3)
    kh = k.transpose(0, 2, 1, 3)
    vh = v.transpose(0, 2, 1, 3).astype(jnp.float32)
    outs = []
    for i in range(s_len // Q_BLOCK):
        start = i * Q_BLOCK
        end = start + Q_BLOCK
        z = jnp.einsum('bhqd,bhkd->bhqk', qh[:, :, start:end], kh[:, :, :end],
                       preferred_element_type=jnp.float32) * scale
        t_pos = start + jnp.arange(Q_BLOCK)[:, None]
        s_pos = jnp.arange(end)[None, :]
        strict = s_pos < t_pos
        log_keep = jnp.where(strict, jax.nn.log_sigmoid(-z), 0.0)
        between = lax.cumsum(log_keep, axis=3, reverse=True) - log_keep
        a = jnp.where(strict, jnp.exp(jax.nn.log_sigmoid(z) + between), 0.0)
        outs.append(jnp.einsum('bhqk,bhkd->bqhd', a, vh[:, :, :end]))
    o = jnp.concatenate(outs, axis=1)
    return o.reshape(b, s_len, h * dh).astype(q.dtype)


def _rel_index():
    i = np.arange(CHUNK)[:, None]
    kk = np.arange(CA_BAND)[None, :]
    dist = (CA_PREV_CHUNKS - kk // CHUNK) * CHUNK + i - kk % CHUNK
    return np.clip(dist, -REL_CLIP, REL_CLIP) + REL_CLIP


def _chunk_attention(q, k, v, rel_bias):
    b, s_len, h, dh = q.shape
    n_chunks = s_len // CHUNK
    scale = dh ** -0.5
    pad = ((0, 0), (CA_PREV_CHUNKS, 0), (0, 0), (0, 0), (0, 0))
    qc = q.reshape(b, n_chunks, CHUNK, h, dh)
    kc = jnp.pad(k.reshape(b, n_chunks, CHUNK, h, dh), pad)
    vc = jnp.pad(v.reshape(b, n_chunks, CHUNK, h, dh), pad)
    band = jnp.arange(n_chunks)[:, None] + jnp.arange(CA_PREV_CHUNKS + 1)[None, :]
    kb = kc[:, band].reshape(b, n_chunks, CA_BAND, h, dh)
    vb = vc[:, band].reshape(b, n_chunks, CA_BAND, h, dh).astype(jnp.float32)
    scores = jnp.einsum('bcqhd,bckhd->bhcqk', qc, kb,
                        preferred_element_type=jnp.float32) * scale
    bias = rel_bias.astype(jnp.float32)[:, _rel_index()]
    valid = jnp.repeat(band >= CA_PREV_CHUNKS, CHUNK, axis=1)
    scores = jnp.where(valid[None, None, :, None, :], scores + bias[:, None], -jnp.inf)
    p = jax.nn.softmax(scores, axis=-1)
    o = jnp.einsum('bhcqk,bckhd->bcqhd', p, vb)
    return o.reshape(b, s_len, h * dh).astype(q.dtype)


def setup_inputs(seed: int = 0) -> dict:
    key = jax.random.key(seed)
    ks = jax.random.split(key, 13)
    beta = DEEPNORM_BETA
    x = jax.random.normal(ks[0], (BATCH, SEQ, D_MODEL), jnp.float32)
    col_scale = jnp.concatenate([
        jnp.ones((2 * SB_WIDTH,), jnp.float32), jnp.full((SB_WIDTH,), beta, jnp.float32),
        jnp.ones((2 * CA_WIDTH,), jnp.float32), jnp.full((CA_WIDTH,), beta, jnp.float32),
        jnp.ones((2 * D_MODEL,), jnp.float32)])
    w_in = jax.random.normal(ks[1], (D_MODEL, IN_COLS), jnp.float32) * D_MODEL ** -0.5 * col_scale
    b_gate = 0.1 * jax.random.normal(ks[2], (2 * D_MODEL,), jnp.float32)
    w_sb_proj = jax.random.normal(ks[3], (SB_WIDTH, D_MODEL), jnp.float32) * SB_WIDTH ** -0.5 * beta
    w_ca_proj = jax.random.normal(ks[4], (CA_WIDTH, D_MODEL), jnp.float32) * CA_WIDTH ** -0.5 * beta
    rel_bias = 0.2 * jax.random.normal(ks[5], (CA_HEADS, 2 * REL_CLIP + 1), jnp.float32)
    w_out = jax.random.normal(ks[6], (D_MODEL, D_MODEL), jnp.float32) * D_MODEL ** -0.5 * beta
    ln1_g = 1.0 + 0.02 * jax.random.normal(ks[7], (D_MODEL,), jnp.float32)
    ln1_b = 0.02 * jax.random.normal(ks[8], (D_MODEL,), jnp.float32)
    w_mlp_in = jax.random.normal(ks[9], (D_MODEL, D_FF), jnp.float32) * D_MODEL ** -0.5 * beta
    w_mlp_out = jax.random.normal(ks[10], (D_FF, D_MODEL), jnp.float32) * D_FF ** -0.5 * beta
    ln2_g = 1.0 + 0.02 * jax.random.normal(ks[11], (D_MODEL,), jnp.float32)
    ln2_b = 0.02 * jax.random.normal(ks[12], (D_MODEL,), jnp.float32)
    return {"x": x, "w_in": w_in, "b_gate": b_gate, "w_sb_proj": w_sb_proj,
            "w_ca_proj": w_ca_proj, "rel_bias": rel_bias, "w_out": w_out,
            "ln1_g": ln1_g, "ln1_b": ln1_b, "w_mlp_in": w_mlp_in, "w_mlp_out": w_mlp_out,
            "ln2_g": ln2_g, "ln2_b": ln2_b}


def reference(x, w_in, b_gate, w_sb_proj, w_ca_proj, rel_bias, w_out,
              ln1_g, ln1_b, w_mlp_in, w_mlp_out, ln2_g, ln2_b):
    b, s_len, _ = x.shape
    for _layer in range(DEPTH):
        h = x @ w_in
        o = 0
        q_sb = h[..., o:o + SB_WIDTH]; o += SB_WIDTH
        k_sb = h[..., o:o + SB_WIDTH]; o += SB_WIDTH
        v_sb = h[..., o:o + SB_WIDTH]; o += SB_WIDTH
        q_ca = h[..., o:o + CA_WIDTH]; o += CA_WIDTH
        k_ca = h[..., o:o + CA_WIDTH]; o += CA_WIDTH
        v_ca = h[..., o:o + CA_WIDTH]; o += CA_WIDTH
        gate_logits = h[..., o:o + 2 * D_MODEL] + b_gate
        sb_shape = (b, s_len, SB_HEADS, SB_HEAD_DIM)
        ca_shape = (b, s_len, CA_HEADS, CA_HEAD_DIM)
        y_sb = _stick_breaking(q_sb.reshape(sb_shape), k_sb.reshape(sb_shape),
                               v_sb.reshape(sb_shape)) @ w_sb_proj
        y_ca = _chunk_attention(q_ca.reshape(ca_shape), k_ca.reshape(ca_shape),
                                v_ca.reshape(ca_shape), rel_bias) @ w_ca_proj
        gates = jax.nn.sigmoid(gate_logits.astype(jnp.float32))
        merged = (gates[..., :D_MODEL] * y_sb + gates[..., D_MODEL:] * y_ca).astype(x.dtype)
        x = _layer_norm(DEEPNORM_ALPHA * x + merged @ w_out, ln1_g, ln1_b)
        ff = jnp.square(jax.nn.relu(x @ w_mlp_in)) @ w_mlp_out
        x = _layer_norm(DEEPNORM_ALPHA * x + ff, ln2_g, ln2_b)
    return x
```

```python
import functools
import math

import jax
import jax.numpy as jnp
import numpy as np
from jax import lax
from jax.experimental import pallas as pl
from jax.experimental.pallas import tpu as pltpu

F32 = jnp.float32
BF16 = jnp.bfloat16

HEADS = 8
HEAD_DIM = 64
WIDTH = HEADS * HEAD_DIM
PAIRS = HEADS // 2
CHUNK = 64
CA_PREV_CHUNKS = 8
REL_CLIP = 256
DEPTH = 1
ALPHA = (2.0 * DEPTH) ** 0.25
LN_EPS = 1e-5
LOG2E = 1.4426950408889634
QK_SCALE = HEAD_DIM ** -0.5

LANES = 128
VMEM_LIMIT_BYTES = 56 * 1024 * 1024

TOKEN_TILE = 512
SB_Q_BLOCK = 256
SB_K_BLOCK = 128
CA_Q_BLOCK = 128
CA_WINDOW_BLOCKS = 5
MASKED_SCORE = -1e30

_NT = (((1,), (1,)), ((), ()))


def _dot(a, b):
    return jnp.dot(a, b, preferred_element_type=F32)


def _params(n_axes):
    return pltpu.CompilerParams(
        dimension_semantics=("arbitrary",) * n_axes,
        vmem_limit_bytes=VMEM_LIMIT_BYTES)


def _inproj_kernel(x_ref, wn_ref, wt_ref, ksb_ref, qca_ref, vca_ref,
                   qsbT_ref, vsbT_ref, kcaT_ref):
    xb = x_ref[...].astype(BF16)
    n = _dot(xb, wn_ref[...])
    ksb_ref[...] = n[:, :WIDTH].astype(BF16)
    qca_ref[...] = (n[:, WIDTH:2 * WIDTH] * (QK_SCALE * LOG2E)).astype(BF16)
    vca_ref[...] = n[:, 2 * WIDTH:].astype(BF16)
    t = lax.dot_general(wt_ref[...], xb, _NT, preferred_element_type=F32)
    qsbT_ref[0] = (t[:WIDTH] * (QK_SCALE * LOG2E)).astype(BF16)
    for c in range(TOKEN_TILE // LANES):
        cols = slice(c * LANES, (c + 1) * LANES)
        vsbT_ref[0, c] = t[WIDTH:2 * WIDTH, cols].astype(BF16)
        kcaT_ref[0, c] = t[2 * WIDTH:, cols].astype(BF16)


def _in_projection(x2d, w_normal, w_transposed, batch, seq):
    tokens, d_model = x2d.shape
    tm = TOKEN_TILE
    tiles_per_seq = seq // tm
    blocks_per_tile = tm // LANES
    row_spec = pl.BlockSpec((tm, WIDTH), lambda t: (t, 0))
    tr_spec = pl.BlockSpec((1, WIDTH, tm), lambda t: (t // tiles_per_seq, 0, t % tiles_per_seq))
    blk_spec = pl.BlockSpec((1, blocks_per_tile, WIDTH, LANES),
                            lambda t: (t // tiles_per_seq, t % tiles_per_seq, 0, 0))
    row_shape = jax.ShapeDtypeStruct((tokens, WIDTH), BF16)
    tr_shape = jax.ShapeDtypeStruct((batch, WIDTH, seq), BF16)
    blk_shape = jax.ShapeDtypeStruct((batch, seq // LANES, WIDTH, LANES), BF16)
    return pl.pallas_call(
        _inproj_kernel,
        grid=(tokens // tm,),
        in_specs=[pl.BlockSpec((tm, d_model), lambda t: (t, 0)),
                  pl.BlockSpec(w_normal.shape, lambda t: (0, 0)),
                  pl.BlockSpec(w_transposed.shape, lambda t: (0, 0))],
        out_specs=[row_spec, row_spec, row_spec, tr_spec, blk_spec, blk_spec],
        out_shape=[row_shape, row_shape, row_shape, tr_shape, blk_shape, blk_shape],
        compiler_params=_params(1),
        name="in_projection",
    )(x2d, w_normal, w_transposed)


def _sb_step(k_blk, qT_h, vT_blk, u2t, carry, mask):
    zT = _dot(k_blk, qT_h)
    e = jnp.exp2(-jnp.abs(zT))
    spf = jnp.maximum(zT, 0.0) + jnp.log(1.0 + e) * LOG2E
    if mask is not None:
        spf = jnp.where(mask, spf, 0.0)
    hi = spf.astype(BF16)
    lo = (spf - hi.astype(F32)).astype(BF16)
    between = _dot(u2t, jnp.concatenate([hi, lo], axis=0)) + carry
    a = jnp.exp2(zT - spf - between)
    if mask is not None:
        a = jnp.where(mask, a, 0.0)
    pv = _dot(vT_blk, a.astype(BF16))
    new_carry = between[0:1, :] + spf[0:1, :]
    return pv, new_carry


def _sb_kernel(qT_ref, k_ref, vT_ref, u2t_ref, o_ref, acc_ref):
    i = pl.program_id(2)
    qb, kb = SB_Q_BLOCK, SB_K_BLOCK
    blocks_per_q = qb // kb
    qT = qT_ref[0]
    u2t = u2t_ref[...]
    feat = lax.broadcasted_iota(jnp.int32, (LANES, qb), 0)
    key_pos = lax.broadcasted_iota(jnp.int32, (kb, qb), 0)
    qry_pos = lax.broadcasted_iota(jnp.int32, (kb, qb), 1)

    def load_blocks(j):
        start = pl.multiple_of(j * kb, kb)
        return k_ref[0, pl.ds(start, kb), :], vT_ref[0, j]

    for h in range(2):
        head_rows = (feat >= h * HEAD_DIM) & (feat < (h + 1) * HEAD_DIM)
        qT_h = jnp.where(head_rows, qT, jnp.zeros_like(qT))
        acc_ref[h] = jnp.zeros((LANES, qb), F32)
        carry = jnp.zeros((1, qb), F32)
        for d in range(blocks_per_q - 1, -1, -1):
            k_blk, vT_blk = load_blocks(i * blocks_per_q + d)
            mask = (key_pos + d * kb) < qry_pos
            pv, carry = _sb_step(k_blk, qT_h, vT_blk, u2t, carry, mask)
            acc_ref[h] += pv

        def body(jj, carry):
            k_blk, vT_blk = load_blocks(i * blocks_per_q - 1 - jj)
            pv, carry = _sb_step(k_blk, qT_h, vT_blk, u2t, carry, None)
            acc_ref[h] += pv
            return carry

        lax.fori_loop(0, i * blocks_per_q, body, carry)

    oT = jnp.where(feat < HEAD_DIM, acc_ref[0], acc_ref[1])
    o_ref[0] = oT.T.astype(BF16)


def _stick_breaking(qT, k, vT_blocks):
    batch, _, seq = qT.shape
    qb, kb = SB_Q_BLOCK, SB_K_BLOCK
    row = np.arange(kb)[:, None]
    col = np.arange(kb)[None, :]
    ut = (col > row).astype(np.float32)
    u2t = jnp.asarray(np.concatenate([ut, ut], axis=1), BF16)
    return pl.pallas_call(
        _sb_kernel,
        grid=(batch, PAIRS, seq // qb),
        in_specs=[pl.BlockSpec((1, LANES, qb), lambda b, p, i: (b, p, i)),
                  pl.BlockSpec((1, seq, LANES), lambda b, p, i: (b, 0, p)),
                  pl.BlockSpec((1, seq // LANES, LANES, LANES), lambda b, p, i: (b, 0, p, 0)),
                  pl.BlockSpec((kb, 2 * kb), lambda b, p, i: (0, 0))],
        out_specs=pl.BlockSpec((1, qb, LANES), lambda b, p, i: (b, i, p)),
        out_shape=jax.ShapeDtypeStruct((batch, seq, WIDTH), BF16),
        scratch_shapes=[pltpu.VMEM((2, LANES, qb), F32)],
        compiler_params=_params(3),
        name="stick_breaking",
    )(qT, k, vT_blocks, u2t)


def _ca_kernel(q_ref, kT_ref, v_ref, bias_ref, o_ref):
    i = pl.program_id(2)
    qp = q_ref[0]
    lane = lax.broadcasted_iota(jnp.int32, (CA_Q_BLOCK, LANES), 1)
    outs = []
    for h in range(2):
        head_lanes = (lane >= h * HEAD_DIM) & (lane < (h + 1) * HEAD_DIM)
        q_h = jnp.where(head_lanes, qp, jnp.zeros_like(qp))
        scores = []
        blocks = []
        for m in range(CA_WINDOW_BLOCKS):
            c = i - (CA_WINDOW_BLOCKS - 1) + m
            cc = jnp.maximum(c, 0)
            s = _dot(q_h, kT_ref[0, cc]) + bias_ref[h, :, m * LANES:(m + 1) * LANES]
            scores.append(jnp.where(c >= 0, s, MASKED_SCORE))
            blocks.append(cc)
        s_all = jnp.concatenate(scores, axis=1)
        mx = jnp.max(s_all, axis=1, keepdims=True)
        p = jnp.exp2(s_all - mx)
        den = jnp.sum(p, axis=1, keepdims=True)
        pb = p.astype(BF16)
        o_h = jnp.zeros((CA_Q_BLOCK, LANES), F32)
        for m in range(CA_WINDOW_BLOCKS):
            start = pl.multiple_of(blocks[m] * LANES, LANES)
            o_h += _dot(pb[:, m * LANES:(m + 1) * LANES], v_ref[0, pl.ds(start, LANES), :])
        outs.append(o_h / den)
    o_ref[0] = jnp.where(lane < HEAD_DIM, outs[0], outs[1]).astype(BF16)


def _ca_bias_table(rel_bias):
    r = np.arange(CA_Q_BLOCK)[:, None]
    kk = np.arange(CA_WINDOW_BLOCKS * LANES)[None, :]
    dist = (CA_WINDOW_BLOCKS - 1) * LANES + r - kk
    idx = np.clip(dist, -REL_CLIP, REL_CLIP) + REL_CLIP
    q_chunk = r // CHUNK
    w_chunk = kk // CHUNK
    band = (w_chunk >= q_chunk) & (w_chunk <= q_chunk + CA_PREV_CHUNKS)
    table = rel_bias.astype(F32)[:, idx] * LOG2E
    return jnp.where(band[None], table, MASKED_SCORE)


def _chunk_attention(q, kT_blocks, v, bias_table):
    batch, seq, _ = q.shape
    return pl.pallas_call(
        _ca_kernel,
        grid=(batch, PAIRS, seq // CA_Q_BLOCK),
        in_specs=[pl.BlockSpec((1, CA_Q_BLOCK, LANES), lambda b, p, i: (b, i, p)),
                  pl.BlockSpec((1, seq // LANES, LANES, LANES), lambda b, p, i: (b, 0, p, 0)),
                  pl.BlockSpec((1, seq, LANES), lambda b, p, i: (b, 0, p)),
                  pl.BlockSpec((2, CA_Q_BLOCK, CA_WINDOW_BLOCKS * LANES), lambda b, p, i: (p, 0, 0))],
        out_specs=pl.BlockSpec((1, CA_Q_BLOCK, LANES), lambda b, p, i: (b, i, p)),
        out_shape=jax.ShapeDtypeStruct((batch, seq, WIDTH), BF16),
        compiler_params=_params(3),
        name="chunk_attention",
    )(q, kT_blocks, v, bias_table)


def _layer_norm(y, g, b):
    mu = jnp.mean(y, axis=-1, keepdims=True)
    d = y - mu
    var = jnp.mean(d * d, axis=-1, keepdims=True)
    return d * lax.rsqrt(var + LN_EPS) * g + b


def _mix_kernel(x_ref, osb_ref, oca_ref, wg_ref, bg_ref, wsb_ref, wca_ref, wout_ref,
                g_ref, b_ref, o_ref):
    x = x_ref[...]
    d_model = x.shape[-1]
    logits = _dot(x.astype(BF16), wg_ref[...]) + bg_ref[...]
    gates = jax.nn.sigmoid(logits)
    y_sb = _dot(osb_ref[...], wsb_ref[...])
    y_ca = _dot(oca_ref[...], wca_ref[...])
    merged = gates[:, :d_model] * y_sb + gates[:, d_model:] * y_ca
    y = ALPHA * x + _dot(merged.astype(BF16), wout_ref[...])
    o_ref[...] = _layer_norm(y, g_ref[...], b_ref[...])


def _mix(x2d, o_sb, o_ca, wg, bg, wsb, wca, wout, g, b):
    tokens, d_model = x2d.shape
    tm = TOKEN_TILE
    const = lambda a: pl.BlockSpec(a.shape, lambda t: (0,) * a.ndim)
    return pl.pallas_call(
        _mix_kernel,
        grid=(tokens // tm,),
        in_specs=[pl.BlockSpec((tm, d_model), lambda t: (t, 0)),
                  pl.BlockSpec((tm, WIDTH), lambda t: (t, 0)),
                  pl.BlockSpec((tm, WIDTH), lambda t: (t, 0)),
                  const(wg), const(bg), const(wsb), const(wca), const(wout), const(g), const(b)],
        out_specs=pl.BlockSpec((tm, d_model), lambda t: (t, 0)),
        out_shape=jax.ShapeDtypeStruct((tokens, d_model), F32),
        compiler_params=_params(1),
        name="mix_ln1",
    )(x2d, o_sb, o_ca, wg, bg, wsb, wca, wout, g, b)


def _mlp_kernel(x_ref, w1_ref, w2_ref, g_ref, b_ref, o_ref, *, ff_chunk):
    x = x_ref[...]
    xb = x.astype(BF16)
    d_ff = w1_ref.shape[1]
    acc = ALPHA * x
    for c in range(d_ff // ff_chunk):
        cols = slice(c * ff_chunk, (c + 1) * ff_chunk)
        h = jnp.maximum(_dot(xb, w1_ref[:, cols]), 0.0)
        acc += _dot((h * h).astype(BF16), w2_ref[cols, :])
    o_ref[...] = _layer_norm(acc, g_ref[...], b_ref[...])


def _mlp(x2d, w1, w2, g, b):
    tokens, d_model = x2d.shape
    tm = TOKEN_TILE
    const = lambda a: pl.BlockSpec(a.shape, lambda t: (0,) * a.ndim)
    return pl.pallas_call(
        functools.partial(_mlp_kernel, ff_chunk=1024),
        grid=(tokens // tm,),
        in_specs=[pl.BlockSpec((tm, d_model), lambda t: (t, 0)),
                  const(w1), const(w2), const(g), const(b)],
        out_specs=pl.BlockSpec((tm, d_model), lambda t: (t, 0)),
        out_shape=jax.ShapeDtypeStruct((tokens, d_model), F32),
        compiler_params=_params(1),
        name="mlp_ln2",
    )(x2d, w1, w2, g, b)


def kernel(x, w_in, b_gate, w_sb_proj, w_ca_proj, rel_bias, w_out, ln1_g, ln1_b,
           w_mlp_in, w_mlp_out, ln2_g, ln2_b):
    batch, seq, d_model = x.shape
    assert seq % TOKEN_TILE == 0 and seq % SB_Q_BLOCK == 0 and d_model % LANES == 0
    x2d = x.reshape(batch * seq, d_model)

    w = WIDTH
    col = lambda n: slice(n * w, (n + 1) * w)
    w_normal = jnp.concatenate([w_in[:, col(1)], w_in[:, col(3)], w_in[:, col(5)]], axis=1).astype(BF16)
    w_transposed = jnp.concatenate([w_in[:, col(0)], w_in[:, col(2)], w_in[:, col(4)]], axis=1).T.astype(BF16)
    w_gate = w_in[:, 6 * w:].astype(BF16)

    k_sb, q_ca, v_ca, q_sbT, v_sbT, k_caT = _in_projection(x2d, w_normal, w_transposed, batch, seq)

    o_sb = _stick_breaking(q_sbT, k_sb.reshape(batch, seq, w), v_sbT)
    o_ca = _chunk_attention(q_ca.reshape(batch, seq, w), k_caT, v_ca.reshape(batch, seq, w),
                            _ca_bias_table(rel_bias))

    row = lambda a: a.reshape(1, -1).astype(F32)
    x1 = _mix(x2d, o_sb.reshape(batch * seq, w), o_ca.reshape(batch * seq, w),
              w_gate, row(b_gate), w_sb_proj.astype(BF16), w_ca_proj.astype(BF16),
              w_out.astype(BF16), row(ln1_g), row(ln1_b))
    x2 = _mlp(x1, w_mlp_in.astype(BF16), w_mlp_out.astype(BF16), row(ln2_g), row(ln2_b))
    return x2.reshape(batch, seq, d_model)
```

```python
import functools

import jax
import jax.numpy as jnp
import numpy as np
from jax import lax
from jax.experimental import pallas as pl
from jax.experimental.pallas import tpu as pltpu

F32 = jnp.float32
BF16 = jnp.bfloat16

HEADS = 8
HEAD_DIM = 64
WIDTH = HEADS * HEAD_DIM
PAIRS = HEADS // 2
CHUNK = 64
CA_PREV_CHUNKS = 8
REL_CLIP = 256
DEPTH = 1
ALPHA = (2.0 * DEPTH) ** 0.25
LN_EPS = 1e-5
LOG2E = 1.4426950408889634
QK_SCALE = HEAD_DIM ** -0.5

LANES = 128
VMEM_LIMIT_BYTES = 56 * 1024 * 1024

TOKEN_TILE = 512
Q_BLOCK = 256
K_BLOCK = 128
SB_FIRST_BLOCKS = 4
SB_TAIL_BLOCKS = 2
CA_WINDOW_BLOCKS = 6
MASKED_SCORE = -1e30
SB_DEAD_LOG2 = 150.0

_NT = (((1,), (1,)), ((), ()))


def _dot(a, b):
    return jnp.dot(a, b, preferred_element_type=F32)


def _params(n_axes):
    return pltpu.CompilerParams(
        dimension_semantics=("arbitrary",) * n_axes,
        vmem_limit_bytes=VMEM_LIMIT_BYTES)


def _inproj_kernel(x_ref, wn_ref, wt_ref, ksb_ref, kca_ref,
                   qsbT_ref, qcaT_ref, vsbT_ref, vcaT_ref):
    xb = x_ref[...].astype(BF16)
    n = _dot(xb, wn_ref[...])
    ksb_ref[...] = n[:, :WIDTH].astype(BF16)
    kca_ref[...] = n[:, WIDTH:].astype(BF16)
    t = lax.dot_general(wt_ref[...], xb, _NT, preferred_element_type=F32)
    w = WIDTH
    qsbT_ref[0] = (t[:w] * (QK_SCALE * LOG2E)).astype(BF16)
    qcaT_ref[0] = (t[w:2 * w] * (QK_SCALE * LOG2E)).astype(BF16)
    for c in range(TOKEN_TILE // LANES):
        cols = slice(c * LANES, (c + 1) * LANES)
        vsbT_ref[0, c] = t[2 * w:3 * w, cols].astype(BF16)
        vcaT_ref[0, c] = t[3 * w:, cols].astype(BF16)


def _in_projection(x2d, w_normal, w_transposed, batch, seq):
    tokens, d_model = x2d.shape
    tm = TOKEN_TILE
    tiles_per_seq = seq // tm
    blocks_per_tile = tm // LANES
    row_spec = pl.BlockSpec((tm, WIDTH), lambda t: (t, 0))
    tr_spec = pl.BlockSpec((1, WIDTH, tm), lambda t: (t // tiles_per_seq, 0, t % tiles_per_seq))
    blk_spec = pl.BlockSpec((1, blocks_per_tile, WIDTH, LANES),
                            lambda t: (t // tiles_per_seq, t % tiles_per_seq, 0, 0))
    row_shape = jax.ShapeDtypeStruct((tokens, WIDTH), BF16)
    tr_shape = jax.ShapeDtypeStruct((batch, WIDTH, seq), BF16)
    blk_shape = jax.ShapeDtypeStruct((batch, seq // LANES, WIDTH, LANES), BF16)
    return pl.pallas_call(
        _inproj_kernel,
        grid=(tokens // tm,),
        in_specs=[pl.BlockSpec((tm, d_model), lambda t: (t, 0)),
                  pl.BlockSpec(w_normal.shape, lambda t: (0, 0)),
                  pl.BlockSpec(w_transposed.shape, lambda t: (0, 0))],
        out_specs=[row_spec, row_spec, tr_spec, tr_spec, blk_spec, blk_spec],
        out_shape=[row_shape, row_shape, tr_shape, tr_shape, blk_shape, blk_shape],
        compiler_params=_params(1),
        name="in_projection",
    )(x2d, w_normal, w_transposed)


def _head_queries(qT):
    zeros = jnp.zeros((HEAD_DIM, qT.shape[1]), qT.dtype)
    return [jnp.concatenate([qT[:HEAD_DIM], zeros], axis=0),
            jnp.concatenate([zeros, qT[HEAD_DIM:]], axis=0)]


def _key_block(k_ref, j):
    return k_ref[0, pl.ds(pl.multiple_of(j * K_BLOCK, K_BLOCK), K_BLOCK), :]


def _head_values(vT_ref, blocks, h):
    rows = pl.ds(h * HEAD_DIM, HEAD_DIM)
    return jnp.concatenate([vT_ref[0, j, rows, :] for j in blocks], axis=1)


def _sb_decay(zT, u2t, mask):
    neg_abs = lax.bitcast_convert_type(
        lax.bitcast_convert_type(zT, jnp.uint32) | jnp.uint32(0x80000000), F32)
    spf = jnp.maximum(zT, 0.0) + jnp.log(1.0 + jnp.exp2(neg_abs)) * LOG2E
    if mask is not None:
        spf = jnp.where(mask, spf, 0.0)
    hi = spf.astype(BF16)
    lo = (spf - hi.astype(F32)).astype(BF16)
    cum = _dot(u2t, jnp.concatenate([hi, lo], axis=0))
    return zT - spf, cum, cum[0:1, :] + spf[0:1, :]


def _sb_batch(qT_hs, k_blks, vT_hs, u2t, carries, masks, valid):
    heads = range(len(qT_hs))
    scores = [[_dot(kb, qT_hs[h]) for kb in k_blks] for h in heads]
    parts = [[_sb_decay(z, u2t, m) for z, m in zip(scores[h], masks)] for h in heads]
    outs, new_carries = [], []
    for h in heads:
        carry = carries[h]
        weights = []
        for (log_beta, cum, total), mask, ok in zip(parts[h], masks, valid):
            shift = carry if ok is None else jnp.where(ok, carry, -MASKED_SCORE)
            a = jnp.exp2(log_beta - cum - shift)
            if mask is not None:
                a = jnp.where(mask, a, 0.0)
            weights.append(a.astype(BF16))
            carry = carry + (total if ok is None else jnp.where(ok, total, 0.0))
        outs.append(jnp.concatenate(weights, axis=0))
        new_carries.append(carry)
    return [_dot(vT_hs[h], outs[h]) for h in heads], new_carries


def _sb_kernel(qT_ref, k_ref, vT_ref, u2t_ref, o_ref, acc_ref):
    i = pl.program_id(2)
    qb, kb = Q_BLOCK, K_BLOCK
    per_q = qb // kb
    qT_hs = _head_queries(qT_ref[0])
    u2t = u2t_ref[...]
    key_pos = lax.broadcasted_iota(jnp.int32, (kb, qb), 0)
    qry_pos = lax.broadcasted_iota(jnp.int32, (kb, qb), 1)

    newest = i * per_q + per_q - 1
    n_back = SB_FIRST_BLOCKS - per_q
    has_back = i * per_q >= n_back
    blocks = [jnp.maximum(newest - g, 0) for g in range(SB_FIRST_BLOCKS)]
    masks = [(key_pos + (per_q - 1 - g) * kb) < qry_pos for g in range(per_q)] + [None] * n_back
    valid = [None] * per_q + [has_back] * n_back
    k_blks = [_key_block(k_ref, j) for j in blocks]
    pvs, carries = _sb_batch(qT_hs, k_blks, [_head_values(vT_ref, blocks, h) for h in range(2)],
                             u2t, [jnp.zeros((1, qb), F32)] * 2, masks, valid)
    for h in range(2):
        acc_ref[h] = pvs[h]

    def alive(state):
        j, c0, c1 = state
        return (j >= 0) & (jnp.minimum(jnp.min(c0), jnp.min(c1)) < SB_DEAD_LOG2)

    def tail(state):
        j, c0, c1 = state
        blocks = [j - g for g in range(SB_TAIL_BLOCKS)]
        k_blks = [_key_block(k_ref, jb) for jb in blocks]
        pvs, out = _sb_batch(qT_hs, k_blks, [_head_values(vT_ref, blocks, h) for h in range(2)],
                             u2t, [c0, c1], [None] * SB_TAIL_BLOCKS, [None] * SB_TAIL_BLOCKS)
        for h in range(2):
            acc_ref[h] += pvs[h]
        return j - SB_TAIL_BLOCKS, out[0], out[1]

    lax.while_loop(alive, tail, (newest - SB_FIRST_BLOCKS, carries[0], carries[1]))

    oT = jnp.concatenate([acc_ref[0], acc_ref[1]], axis=0)
    o_ref[0] = oT.T.astype(BF16)


def _attention_specs(seq):
    qb = Q_BLOCK
    in_specs = [pl.BlockSpec((1, LANES, qb), lambda b, p, i: (b, p, i)),
                pl.BlockSpec((1, seq, LANES), lambda b, p, i: (b, 0, p)),
                pl.BlockSpec((1, seq // LANES, LANES, LANES), lambda b, p, i: (b, 0, p, 0))]
    out_spec = pl.BlockSpec((1, qb, LANES), lambda b, p, i: (b, i, p))
    return in_specs, out_spec


def _stick_breaking(qT, k, vT_blocks):
    batch, _, seq = qT.shape
    kb = K_BLOCK
    assert (seq // kb - SB_FIRST_BLOCKS) % SB_TAIL_BLOCKS == 0 and Q_BLOCK % kb == 0
    row = np.arange(kb)[:, None]
    col = np.arange(kb)[None, :]
    ut = (col > row).astype(np.float32)
    u2t = jnp.asarray(np.concatenate([ut, ut], axis=1), BF16)
    in_specs, out_spec = _attention_specs(seq)
    return pl.pallas_call(
        _sb_kernel,
        grid=(batch, PAIRS, seq // Q_BLOCK),
        in_specs=in_specs + [pl.BlockSpec((kb, 2 * kb), lambda b, p, i: (0, 0))],
        out_specs=out_spec,
        out_shape=jax.ShapeDtypeStruct((batch, seq, WIDTH), BF16),
        scratch_shapes=[pltpu.VMEM((2, HEAD_DIM, Q_BLOCK), F32)],
        compiler_params=_params(3),
        name="stick_breaking",
    )(qT, k, vT_blocks, u2t)


def _ca_kernel(qT_ref, k_ref, vT_ref, bias_ref, o_ref):
    i = pl.program_id(2)
    qb, kb = Q_BLOCK, K_BLOCK
    qT_hs = _head_queries(qT_ref[0])
    first = i * (qb // kb) - (CA_WINDOW_BLOCKS - qb // kb)
    blocks = [jnp.maximum(first + m, 0) for m in range(CA_WINDOW_BLOCKS)]
    k_blks = [_key_block(k_ref, j) for j in blocks]
    heads = range(2)
    raw = [[_dot(k_blks[m], qT_hs[h]) for m in range(CA_WINDOW_BLOCKS)] for h in heads]
    probs, dens = [], []
    for h in heads:
        scores = []
        for m in range(CA_WINDOW_BLOCKS):
            s = raw[h][m] + bias_ref[h, m * kb:(m + 1) * kb, :]
            if m < CA_WINDOW_BLOCKS - qb // kb:
                s = jnp.where(first + m >= 0, s, MASKED_SCORE)
            scores.append(s)
        sT = jnp.concatenate(scores, axis=0)
        mx = jnp.max(sT, axis=0, keepdims=True)
        p = jnp.exp2(sT - mx)
        dens.append(jnp.sum(p, axis=0, keepdims=True))
        probs.append(p.astype(BF16))
    outs = [_dot(_head_values(vT_ref, blocks, h), probs[h]) / dens[h] for h in heads]
    o_ref[0] = jnp.concatenate(outs, axis=0).T.astype(BF16)


def _ca_bias_table(rel_bias):
    n_keys = CA_WINDOW_BLOCKS * K_BLOCK
    heads = rel_bias.shape[0]
    lo_dist = (n_keys - Q_BLOCK) - (n_keys - 1)
    assert lo_dist >= -REL_CLIP
    span = n_keys + Q_BLOCK - 1
    head_part = rel_bias[:, REL_CLIP + lo_dist:].astype(F32)
    tail_len = span - head_part.shape[1]
    ext = jnp.concatenate([head_part, jnp.broadcast_to(rel_bias[:, -1:].astype(F32), (heads, tail_len))], axis=1)
    flat = jnp.broadcast_to(ext[:, None, :], (heads, n_keys, span)).reshape(heads, n_keys * span)
    table = flat[:, n_keys - 1:n_keys - 1 + n_keys * (span - 1)].reshape(heads, n_keys, span - 1)[:, :, :Q_BLOCK]
    kk = np.arange(n_keys)[:, None] // CHUNK
    qc = np.arange(Q_BLOCK)[None, :] // CHUNK
    band = (kk >= qc) & (kk <= qc + CA_PREV_CHUNKS)
    return jnp.where(band[None], table * LOG2E, MASKED_SCORE)


def _chunk_attention(qT, k, vT_blocks, bias_table):
    batch, _, seq = qT.shape
    in_specs, out_spec = _attention_specs(seq)
    n_keys = CA_WINDOW_BLOCKS * K_BLOCK
    assert n_keys - Q_BLOCK == CA_PREV_CHUNKS * CHUNK and Q_BLOCK % CHUNK == 0
    return pl.pallas_call(
        _ca_kernel,
        grid=(batch, PAIRS, seq // Q_BLOCK),
        in_specs=in_specs + [pl.BlockSpec((2, n_keys, Q_BLOCK), lambda b, p, i: (p, 0, 0))],
        out_specs=out_spec,
        out_shape=jax.ShapeDtypeStruct((batch, seq, WIDTH), BF16),
        compiler_params=_params(3),
        name="chunk_attention",
    )(qT, k, vT_blocks, bias_table)


def _layer_norm(y, g, b):
    mu = jnp.mean(y, axis=-1, keepdims=True)
    d = y - mu
    var = jnp.mean(d * d, axis=-1, keepdims=True)
    return d * lax.rsqrt(var + LN_EPS) * g + b


def _mix_kernel(x_ref, osb_ref, oca_ref, wg_ref, bg_ref, wsb_ref, wca_ref, wout_ref,
                g_ref, b_ref, o_ref):
    x = x_ref[...]
    d_model = x.shape[-1]
    logits = _dot(x.astype(BF16), wg_ref[...]) + bg_ref[...]
    gates = jax.nn.sigmoid(logits)
    y_sb = _dot(osb_ref[...], wsb_ref[...])
    y_ca = _dot(oca_ref[...], wca_ref[...])
    merged = gates[:, :d_model] * y_sb + gates[:, d_model:] * y_ca
    y = ALPHA * x + _dot(merged.astype(BF16), wout_ref[...])
    o_ref[...] = _layer_norm(y, g_ref[...], b_ref[...])


def _mix(x2d, o_sb, o_ca, wg, bg, wsb, wca, wout, g, b):
    tokens, d_model = x2d.shape
    tm = TOKEN_TILE
    const = lambda a: pl.BlockSpec(a.shape, lambda t: (0,) * a.ndim)
    return pl.pallas_call(
        _mix_kernel,
        grid=(tokens // tm,),
        in_specs=[pl.BlockSpec((tm, d_model), lambda t: (t, 0)),
                  pl.BlockSpec((tm, WIDTH), lambda t: (t, 0)),
                  pl.BlockSpec((tm, WIDTH), lambda t: (t, 0)),
                  const(wg), const(bg), const(wsb), const(wca), const(wout), const(g), const(b)],
        out_specs=pl.BlockSpec((tm, d_model), lambda t: (t, 0)),
        out_shape=jax.ShapeDtypeStruct((tokens, d_model), F32),
        compiler_params=_params(1),
        name="mix_ln1",
    )(x2d, o_sb, o_ca, wg, bg, wsb, wca, wout, g, b)


def _mlp_kernel(x_ref, w1_ref, w2_ref, g_ref, b_ref, o_ref, *, ff_chunk):
    x = x_ref[...]
    xb = x.astype(BF16)
    d_ff = w1_ref.shape[1]
    acc = ALPHA * x
    for c in range(d_ff // ff_chunk):
        cols = slice(c * ff_chunk, (c + 1) * ff_chunk)
        h = jnp.maximum(_dot(xb, w1_ref[:, cols]), 0.0)
        acc += _dot((h * h).astype(BF16), w2_ref[cols, :])
    o_ref[...] = _layer_norm(acc, g_ref[...], b_ref[...])


def _mlp(x2d, w1, w2, g, b):
    tokens, d_model = x2d.shape
    tm = TOKEN_TILE
    const = lambda a: pl.BlockSpec(a.shape, lambda t: (0,) * a.ndim)
    return pl.pallas_call(
        functools.partial(_mlp_kernel, ff_chunk=1024),
        grid=(tokens // tm,),
        in_specs=[pl.BlockSpec((tm, d_model), lambda t: (t, 0)),
                  const(w1), const(w2), const(g), const(b)],
        out_specs=pl.BlockSpec((tm, d_model), lambda t: (t, 0)),
        out_shape=jax.ShapeDtypeStruct((tokens, d_model), F32),
        compiler_params=_params(1),
        name="mlp_ln2",
    )(x2d, w1, w2, g, b)


def kernel(x, w_in, b_gate, w_sb_proj, w_ca_proj, rel_bias, w_out, ln1_g, ln1_b,
           w_mlp_in, w_mlp_out, ln2_g, ln2_b):
    batch, seq, d_model = x.shape
    assert seq % TOKEN_TILE == 0 and seq % Q_BLOCK == 0 and d_model % LANES == 0
    x2d = x.reshape(batch * seq, d_model)

    w = WIDTH
    col = lambda n: w_in[:, n * w:(n + 1) * w]
    w_normal = jnp.concatenate([col(1), col(4)], axis=1).astype(BF16)
    w_transposed = jnp.concatenate([col(0), col(3), col(2), col(5)], axis=1).T.astype(BF16)
    w_gate = w_in[:, 6 * w:].astype(BF16)

    k_sb, k_ca, q_sbT, q_caT, v_sbT, v_caT = _in_projection(x2d, w_normal, w_transposed, batch, seq)

    o_sb = _stick_breaking(q_sbT, k_sb.reshape(batch, seq, w), v_sbT)
    o_ca = _chunk_attention(q_caT, k_ca.reshape(batch, seq, w), v_caT, _ca_bias_table(rel_bias))

    row = lambda a: a.reshape(1, -1).astype(F32)
    x1 = _mix(x2d, o_sb.reshape(batch * seq, w), o_ca.reshape(batch * seq, w),
              w_gate, row(b_gate), w_sb_proj.astype(BF16), w_ca_proj.astype(BF16),
              w_out.astype(BF16), row(ln1_g), row(ln1_b))
    x2 = _mlp(x1, w_mlp_in.astype(BF16), w_mlp_out.astype(BF16), row(ln2_g), row(ln2_b))
    return x2.reshape(batch, seq, d_model)
```

```python
import functools

import jax
import jax.numpy as jnp
import numpy as np
from jax import lax
from jax.experimental import pallas as pl
from jax.experimental.pallas import tpu as pltpu

F32 = jnp.float32
BF16 = jnp.bfloat16

HEADS = 8
HEAD_DIM = 64
WIDTH = HEADS * HEAD_DIM
PAIRS = HEADS // 2
CHUNK = 64
CA_PREV_CHUNKS = 8
REL_CLIP = 256
DEPTH = 1
ALPHA = (2.0 * DEPTH) ** 0.25
LN_EPS = 1e-5
LOG2E = 1.4426950408889634
QK_SCALE = HEAD_DIM ** -0.5

LANES = 128
VMEM_LIMIT_BYTES = 56 * 1024 * 1024

TOKEN_TILE = 512
Q_BLOCK = 256
K_BLOCK = 128
SB_FIRST_BLOCKS = 4
SB_TAIL_BLOCKS = 2
CA_WINDOW_BLOCKS = 6
MASKED_SCORE = -1e30
SB_DEAD_LOG2 = 150.0

_NT = (((1,), (1,)), ((), ()))


def _dot(a, b):
    return jnp.dot(a, b, preferred_element_type=F32)


def _params(n_axes):
    return pltpu.CompilerParams(
        dimension_semantics=("arbitrary",) * n_axes,
        vmem_limit_bytes=VMEM_LIMIT_BYTES)


def _inproj_kernel(x_ref, wn_ref, wt_ref, ksb_ref, kca_ref,
                   qsbT_ref, qcaT_ref, vsbT_ref, vcaT_ref):
    xb = x_ref[...].astype(BF16)
    n = _dot(xb, wn_ref[...])
    ksb_ref[...] = n[:, :WIDTH].astype(BF16)
    kca_ref[...] = n[:, WIDTH:].astype(BF16)
    t = lax.dot_general(wt_ref[...], xb, _NT, preferred_element_type=F32)
    w = WIDTH
    qsbT_ref[0] = (t[:w] * (QK_SCALE * LOG2E)).astype(BF16)
    qcaT_ref[0] = (t[w:2 * w] * (QK_SCALE * LOG2E)).astype(BF16)
    for c in range(TOKEN_TILE // LANES):
        cols = slice(c * LANES, (c + 1) * LANES)
        vsbT_ref[0, c] = t[2 * w:3 * w, cols].astype(BF16)
        vcaT_ref[0, c] = t[3 * w:, cols].astype(BF16)


def _in_projection(x2d, w_normal, w_transposed, batch, seq):
    tokens, d_model = x2d.shape
    tm = TOKEN_TILE
    tiles_per_seq = seq // tm
    blocks_per_tile = tm // LANES
    row_spec = pl.BlockSpec((tm, WIDTH), lambda t: (t, 0))
    tr_spec = pl.BlockSpec((1, WIDTH, tm), lambda t: (t // tiles_per_seq, 0, t % tiles_per_seq))
    blk_spec = pl.BlockSpec((1, blocks_per_tile, WIDTH, LANES),
                            lambda t: (t // tiles_per_seq, t % tiles_per_seq, 0, 0))
    row_shape = jax.ShapeDtypeStruct((tokens, WIDTH), BF16)
    tr_shape = jax.ShapeDtypeStruct((batch, WIDTH, seq), BF16)
    blk_shape = jax.ShapeDtypeStruct((batch, seq // LANES, WIDTH, LANES), BF16)
    return pl.pallas_call(
        _inproj_kernel,
        grid=(tokens // tm,),
        in_specs=[pl.BlockSpec((tm, d_model), lambda t: (t, 0)),
                  pl.BlockSpec(w_normal.shape, lambda t: (0, 0)),
                  pl.BlockSpec(w_transposed.shape, lambda t: (0, 0))],
        out_specs=[row_spec, row_spec, tr_spec, tr_spec, blk_spec, blk_spec],
        out_shape=[row_shape, row_shape, tr_shape, tr_shape, blk_shape, blk_shape],
        compiler_params=_params(1),
        name="in_projection",
    )(x2d, w_normal, w_transposed)


def _head_queries(qT):
    zeros = jnp.zeros((HEAD_DIM, qT.shape[1]), qT.dtype)
    return [jnp.concatenate([qT[:HEAD_DIM], zeros], axis=0),
            jnp.concatenate([zeros, qT[HEAD_DIM:]], axis=0)]


def _key_block(k_ref, j):
    return k_ref[0, pl.ds(pl.multiple_of(j * K_BLOCK, K_BLOCK), K_BLOCK), :]


def _head_values(vT_ref, blocks, h):
    rows = pl.ds(h * HEAD_DIM, HEAD_DIM)
    return jnp.concatenate([vT_ref[0, j, rows, :] for j in blocks], axis=1)


def _sb_decay(zT, u2t, mask):
    spf = jnp.maximum(zT, 0.0) + jnp.log(1.0 + jnp.exp2(-jnp.abs(zT))) * LOG2E
    if mask is not None:
        spf = jnp.where(mask, spf, 0.0)
    hi = spf.astype(BF16)
    lo = (spf - hi.astype(F32)).astype(BF16)
    cum = _dot(u2t, jnp.concatenate([hi, lo], axis=0))
    return zT - spf, cum, cum[0:1, :] + spf[0:1, :]


def _sb_batch(qT_hs, k_blks, vT_hs, u2t, carries, masks, valid):
    heads = range(len(qT_hs))
    scores = [[_dot(kb, qT_hs[h]) for kb in k_blks] for h in heads]
    parts = [[_sb_decay(z, u2t, m) for z, m in zip(scores[h], masks)] for h in heads]
    outs, new_carries = [], []
    for h in heads:
        carry = carries[h]
        weights = []
        for (log_beta, cum, total), mask, ok in zip(parts[h], masks, valid):
            shift = carry if ok is None else jnp.where(ok, carry, -MASKED_SCORE)
            a = jnp.exp2(log_beta - cum - shift)
            if mask is not None:
                a = jnp.where(mask, a, 0.0)
            weights.append(a.astype(BF16))
            carry = carry + (total if ok is None else jnp.where(ok, total, 0.0))
        outs.append(jnp.concatenate(weights, axis=0))
        new_carries.append(carry)
    return [_dot(vT_hs[h], outs[h]) for h in heads], new_carries


def _sb_kernel(qT_ref, k_ref, vT_ref, u2t_ref, o_ref, acc_ref):
    i = pl.program_id(2)
    qb, kb = Q_BLOCK, K_BLOCK
    per_q = qb // kb
    qT_hs = _head_queries(qT_ref[0])
    u2t = u2t_ref[...]
    key_pos = lax.broadcasted_iota(jnp.int32, (kb, qb), 0)
    qry_pos = lax.broadcasted_iota(jnp.int32, (kb, qb), 1)

    newest = i * per_q + per_q - 1
    n_back = SB_FIRST_BLOCKS - per_q
    has_back = i * per_q >= n_back
    blocks = [jnp.maximum(newest - g, 0) for g in range(SB_FIRST_BLOCKS)]
    masks = [(key_pos + (per_q - 1 - g) * kb) < qry_pos for g in range(per_q)] + [None] * n_back
    valid = [None] * per_q + [has_back] * n_back
    k_blks = [_key_block(k_ref, j) for j in blocks]
    pvs, carries = _sb_batch(qT_hs, k_blks, [_head_values(vT_ref, blocks, h) for h in range(2)],
                             u2t, [jnp.zeros((1, qb), F32)] * 2, masks, valid)
    for h in range(2):
        acc_ref[h] = pvs[h]

    def alive(state):
        j, c0, c1 = state
        return (j >= 0) & (jnp.minimum(jnp.min(c0), jnp.min(c1)) < SB_DEAD_LOG2)

    def tail(state):
        j, c0, c1 = state
        blocks = [j - g for g in range(SB_TAIL_BLOCKS)]
        k_blks = [_key_block(k_ref, jb) for jb in blocks]
        pvs, out = _sb_batch(qT_hs, k_blks, [_head_values(vT_ref, blocks, h) for h in range(2)],
                             u2t, [c0, c1], [None] * SB_TAIL_BLOCKS, [None] * SB_TAIL_BLOCKS)
        for h in range(2):
            acc_ref[h] += pvs[h]
        return j - SB_TAIL_BLOCKS, out[0], out[1]

    lax.while_loop(alive, tail, (newest - SB_FIRST_BLOCKS, carries[0], carries[1]))

    oT = jnp.concatenate([acc_ref[0], acc_ref[1]], axis=0)
    o_ref[0] = oT.T.astype(BF16)


def _attention_specs(seq):
    qb = Q_BLOCK
    in_specs = [pl.BlockSpec((1, LANES, qb), lambda b, p, i: (b, p, i)),
                pl.BlockSpec((1, seq, LANES), lambda b, p, i: (b, 0, p)),
                pl.BlockSpec((1, seq // LANES, LANES, LANES), lambda b, p, i: (b, 0, p, 0))]
    out_spec = pl.BlockSpec((1, qb, LANES), lambda b, p, i: (b, i, p))
    return in_specs, out_spec


def _stick_breaking(qT, k, vT_blocks):
    batch, _, seq = qT.shape
    kb = K_BLOCK
    assert (seq // kb - SB_FIRST_BLOCKS) % SB_TAIL_BLOCKS == 0 and Q_BLOCK % kb == 0
    row = np.arange(kb)[:, None]
    col = np.arange(kb)[None, :]
    ut = (col > row).astype(np.float32)
    u2t = jnp.asarray(np.concatenate([ut, ut], axis=1), BF16)
    in_specs, out_spec = _attention_specs(seq)
    return pl.pallas_call(
        _sb_kernel,
        grid=(batch, PAIRS, seq // Q_BLOCK),
        in_specs=in_specs + [pl.BlockSpec((kb, 2 * kb), lambda b, p, i: (0, 0))],
        out_specs=out_spec,
        out_shape=jax.ShapeDtypeStruct((batch, seq, WIDTH), BF16),
        scratch_shapes=[pltpu.VMEM((2, HEAD_DIM, Q_BLOCK), F32)],
        compiler_params=_params(3),
        name="stick_breaking",
    )(qT, k, vT_blocks, u2t)


def _ca_kernel(qT_ref, k_ref, vT_ref, bias_ref, o_ref):
    i = pl.program_id(2)
    qb, kb = Q_BLOCK, K_BLOCK
    qT_hs = _head_queries(qT_ref[0])
    first = i * (qb // kb) - (CA_WINDOW_BLOCKS - qb // kb)
    blocks = [jnp.maximum(first + m, 0) for m in range(CA_WINDOW_BLOCKS)]
    k_blks = [_key_block(k_ref, j) for j in blocks]
    heads = range(2)
    raw = [[_dot(k_blks[m], qT_hs[h]) for m in range(CA_WINDOW_BLOCKS)] for h in heads]
    probs, dens = [], []
    for h in heads:
        scores = []
        for m in range(CA_WINDOW_BLOCKS):
            s = raw[h][m] + bias_ref[h, m * kb:(m + 1) * kb, :]
            if m < CA_WINDOW_BLOCKS - qb // kb:
                s = jnp.where(first + m >= 0, s, MASKED_SCORE)
            scores.append(s)
        sT = jnp.concatenate(scores, axis=0)
        mx = jnp.max(sT, axis=0, keepdims=True)
        p = jnp.exp2(sT - mx)
        dens.append(jnp.sum(p, axis=0, keepdims=True))
        probs.append(p.astype(BF16))
    outs = [_dot(_head_values(vT_ref, blocks, h), probs[h]) / dens[h] for h in heads]
    o_ref[0] = jnp.concatenate(outs, axis=0).T.astype(BF16)


def _ca_bias_kernel(ext_ref, o_ref):
    n_keys, span = o_ref.shape[1], ext_ref.shape[2]
    rows = jnp.broadcast_to(ext_ref[0], (n_keys, span))
    table = pltpu.roll(rows, span - (n_keys - 1), axis=1, stride=1, stride_axis=0)[:, :Q_BLOCK]
    key_chunk = lax.broadcasted_iota(jnp.int32, (n_keys, Q_BLOCK), 0) // CHUNK
    qry_chunk = lax.broadcasted_iota(jnp.int32, (n_keys, Q_BLOCK), 1) // CHUNK
    band = (key_chunk >= qry_chunk) & (key_chunk <= qry_chunk + CA_PREV_CHUNKS)
    o_ref[0] = jnp.where(band, table * LOG2E, MASKED_SCORE)


def _ca_bias_table(rel_bias):
    n_keys = CA_WINDOW_BLOCKS * K_BLOCK
    heads = rel_bias.shape[0]
    lo_dist = (n_keys - Q_BLOCK) - (n_keys - 1)
    assert lo_dist >= -REL_CLIP
    span = pl.next_power_of_2(n_keys + Q_BLOCK - 1)
    head_part = rel_bias[:, REL_CLIP + lo_dist:].astype(F32)
    tail = jnp.broadcast_to(rel_bias[:, -1:].astype(F32), (heads, span - head_part.shape[1]))
    ext = jnp.concatenate([head_part, tail], axis=1).reshape(heads, 1, span)
    return pl.pallas_call(
        _ca_bias_kernel,
        grid=(heads,),
        in_specs=[pl.BlockSpec((1, 1, span), lambda h: (h, 0, 0))],
        out_specs=pl.BlockSpec((1, n_keys, Q_BLOCK), lambda h: (h, 0, 0)),
        out_shape=jax.ShapeDtypeStruct((heads, n_keys, Q_BLOCK), F32),
        compiler_params=_params(1),
        name="ca_bias_table",
    )(ext)


def _chunk_attention(qT, k, vT_blocks, bias_table):
    batch, _, seq = qT.shape
    in_specs, out_spec = _attention_specs(seq)
    n_keys = CA_WINDOW_BLOCKS * K_BLOCK
    assert n_keys - Q_BLOCK == CA_PREV_CHUNKS * CHUNK and Q_BLOCK % CHUNK == 0
    return pl.pallas_call(
        _ca_kernel,
        grid=(batch, PAIRS, seq // Q_BLOCK),
        in_specs=in_specs + [pl.BlockSpec((2, n_keys, Q_BLOCK), lambda b, p, i: (p, 0, 0))],
        out_specs=out_spec,
        out_shape=jax.ShapeDtypeStruct((batch, seq, WIDTH), BF16),
        compiler_params=_params(3),
        name="chunk_attention",
    )(qT, k, vT_blocks, bias_table)


def _layer_norm(y, g, b):
    mu = jnp.mean(y, axis=-1, keepdims=True)
    d = y - mu
    var = jnp.mean(d * d, axis=-1, keepdims=True)
    return d * lax.rsqrt(var + LN_EPS) * g + b


def _mix_kernel(x_ref, osb_ref, oca_ref, wg_ref, bg_ref, wsb_ref, wca_ref, wout_ref,
                g_ref, b_ref, o_ref):
    x = x_ref[...]
    d_model = x.shape[-1]
    logits = _dot(x.astype(BF16), wg_ref[...]) + bg_ref[...]
    gates = jax.nn.sigmoid(logits)
    y_sb = _dot(osb_ref[...], wsb_ref[...])
    y_ca = _dot(oca_ref[...], wca_ref[...])
    merged = gates[:, :d_model] * y_sb + gates[:, d_model:] * y_ca
    y = ALPHA * x + _dot(merged.astype(BF16), wout_ref[...])
    o_ref[...] = _layer_norm(y, g_ref[...], b_ref[...])


def _mix(x2d, o_sb, o_ca, wg, bg, wsb, wca, wout, g, b):
    tokens, d_model = x2d.shape
    tm = TOKEN_TILE
    const = lambda a: pl.BlockSpec(a.shape, lambda t: (0,) * a.ndim)
    return pl.pallas_call(
        _mix_kernel,
        grid=(tokens // tm,),
        in_specs=[pl.BlockSpec((tm, d_model), lambda t: (t, 0)),
                  pl.BlockSpec((tm, WIDTH), lambda t: (t, 0)),
                  pl.BlockSpec((tm, WIDTH), lambda t: (t, 0)),
                  const(wg), const(bg), const(wsb), const(wca), const(wout), const(g), const(b)],
        out_specs=pl.BlockSpec((tm, d_model), lambda t: (t, 0)),
        out_shape=jax.ShapeDtypeStruct((tokens, d_model), F32),
        compiler_params=_params(1),
        name="mix_ln1",
    )(x2d, o_sb, o_ca, wg, bg, wsb, wca, wout, g, b)


def _mlp_kernel(x_ref, w1_ref, w2_ref, g_ref, b_ref, o_ref, *, ff_chunk):
    x = x_ref[...]
    xb = x.astype(BF16)
    d_ff = w1_ref.shape[1]
    acc = ALPHA * x
    for c in range(d_ff // ff_chunk):
        cols = slice(c * ff_chunk, (c + 1) * ff_chunk)
        h = jnp.maximum(_dot(xb, w1_ref[:, cols]), 0.0)
        acc += _dot((h * h).astype(BF16), w2_ref[cols, :])
    o_ref[...] = _layer_norm(acc, g_ref[...], b_ref[...])


def _mlp(x2d, w1, w2, g, b):
    tokens, d_model = x2d.shape
    tm = TOKEN_TILE
    const = lambda a: pl.BlockSpec(a.shape, lambda t: (0,) * a.ndim)
    return pl.pallas_call(
        functools.partial(_mlp_kernel, ff_chunk=1024),
        grid=(tokens // tm,),
        in_specs=[pl.BlockSpec((tm, d_model), lambda t: (t, 0)),
                  const(w1), const(w2), const(g), const(b)],
        out_specs=pl.BlockSpec((tm, d_model), lambda t: (t, 0)),
        out_shape=jax.ShapeDtypeStruct((tokens, d_model), F32),
        compiler_params=_params(1),
        name="mlp_ln2",
    )(x2d, w1, w2, g, b)


def kernel(x, w_in, b_gate, w_sb_proj, w_ca_proj, rel_bias, w_out, ln1_g, ln1_b,
           w_mlp_in, w_mlp_out, ln2_g, ln2_b):
    batch, seq, d_model = x.shape
    assert seq % TOKEN_TILE == 0 and seq % Q_BLOCK == 0 and d_model % LANES == 0
    x2d = x.reshape(batch * seq, d_model)

    w = WIDTH
    col = lambda n: w_in[:, n * w:(n + 1) * w]
    w_normal = jnp.concatenate([col(1), col(4)], axis=1).astype(BF16)
    w_transposed = jnp.concatenate([col(0), col(3), col(2), col(5)], axis=1).T.astype(BF16)
    w_gate = w_in[:, 6 * w:].astype(BF16)

    k_sb, k_ca, q_sbT, q_caT, v_sbT, v_caT = _in_projection(x2d, w_normal, w_transposed, batch, seq)

    o_sb = _stick_breaking(q_sbT, k_sb.reshape(batch, seq, w), v_sbT)
    o_ca = _chunk_attention(q_caT, k_ca.reshape(batch, seq, w), v_caT, _ca_bias_table(rel_bias))

    row = lambda a: a.reshape(1, -1).astype(F32)
    x1 = _mix(x2d, o_sb.reshape(batch * seq, w), o_ca.reshape(batch * seq, w),
              w_gate, row(b_gate), w_sb_proj.astype(BF16), w_ca_proj.astype(BF16),
              w_out.astype(BF16), row(ln1_g), row(ln1_b))
    x2 = _mlp(x1, w_mlp_in.astype(BF16), w_mlp_out.astype(BF16), row(ln2_g), row(ln2_b))
    return x2.reshape(batch, seq, d_model)
```

```python
import functools

import jax
import jax.numpy as jnp
import numpy as np
from jax import lax
from jax.experimental import pallas as pl
from jax.experimental.pallas import tpu as pltpu

F32 = jnp.float32
BF16 = jnp.bfloat16

HEADS = 8
HEAD_DIM = 64
WIDTH = HEADS * HEAD_DIM
STEP_HEADS = 4
CHUNK = 64
CA_PREV_CHUNKS = 8
REL_CLIP = 256
DEPTH = 1
ALPHA = (2.0 * DEPTH) ** 0.25
LN_EPS = 1e-5
LOG2E = 1.4426950408889634
QK_SCALE = HEAD_DIM ** -0.5

LANES = 128
VMEM_LIMIT_BYTES = 56 * 1024 * 1024

TOKEN_TILE = 512
Q_BLOCK = 256
K_BLOCK = 128
SB_FIRST_BLOCKS = 4
SB_TAIL_BLOCKS = 2
CA_WINDOW_BLOCKS = 6
MASKED_SCORE = -1e30
SB_DEAD_LOG2 = 150.0
EXP2_CLAMP = 126.0

_NT = (((1,), (1,)), ((), ()))


def _dot(a, b):
    return jnp.dot(a, b, preferred_element_type=F32)


def _params(n_axes):
    return pltpu.CompilerParams(
        dimension_semantics=("arbitrary",) * n_axes,
        vmem_limit_bytes=VMEM_LIMIT_BYTES)


def _inproj_kernel(x_ref, wn_ref, wt_ref, ksb_ref, kca_ref,
                   qsbT_ref, qcaT_ref, vsbT_ref, vcaT_ref):
    xb = x_ref[...].astype(BF16)
    n = _dot(xb, wn_ref[...])
    ksb_ref[...] = n[:, :WIDTH].astype(BF16)
    kca_ref[...] = n[:, WIDTH:].astype(BF16)
    t = lax.dot_general(wt_ref[...], xb, _NT, preferred_element_type=F32)
    w = WIDTH
    qsbT_ref[0] = (t[:w] * (QK_SCALE * LOG2E)).astype(BF16)
    qcaT_ref[0] = (t[w:2 * w] * (QK_SCALE * LOG2E)).astype(BF16)
    for c in range(TOKEN_TILE // LANES):
        cols = slice(c * LANES, (c + 1) * LANES)
        vsbT_ref[0, c] = t[2 * w:3 * w, cols].astype(BF16)
        vcaT_ref[0, c] = t[3 * w:, cols].astype(BF16)


def _in_projection(x2d, w_normal, w_transposed, batch, seq):
    tokens, d_model = x2d.shape
    tm = TOKEN_TILE
    tiles_per_seq = seq // tm
    blocks_per_tile = tm // LANES
    row_spec = pl.BlockSpec((tm, WIDTH), lambda t: (t, 0))
    tr_spec = pl.BlockSpec((1, WIDTH, tm), lambda t: (t // tiles_per_seq, 0, t % tiles_per_seq))
    blk_spec = pl.BlockSpec((1, blocks_per_tile, WIDTH, LANES),
                            lambda t: (t // tiles_per_seq, t % tiles_per_seq, 0, 0))
    row_shape = jax.ShapeDtypeStruct((tokens, WIDTH), BF16)
    tr_shape = jax.ShapeDtypeStruct((batch, WIDTH, seq), BF16)
    blk_shape = jax.ShapeDtypeStruct((batch, seq // LANES, WIDTH, LANES), BF16)
    return pl.pallas_call(
        _inproj_kernel,
        grid=(tokens // tm,),
        in_specs=[pl.BlockSpec((tm, d_model), lambda t: (t, 0)),
                  pl.BlockSpec(w_normal.shape, lambda t: (0, 0)),
                  pl.BlockSpec(w_transposed.shape, lambda t: (0, 0))],
        out_specs=[row_spec, row_spec, tr_spec, tr_spec, blk_spec, blk_spec],
        out_shape=[row_shape, row_shape, tr_shape, tr_shape, blk_shape, blk_shape],
        compiler_params=_params(1),
        name="in_projection",
    )(x2d, w_normal, w_transposed)


def _head_queries(qT):
    zeros = jnp.zeros((HEAD_DIM, qT.shape[1]), qT.dtype)
    out = []
    for h in range(STEP_HEADS):
        rows = qT[h * HEAD_DIM:(h + 1) * HEAD_DIM]
        out.append(jnp.concatenate([rows, zeros] if h % 2 == 0 else [zeros, rows], axis=0))
    return out


def _key_block(k_ref, j):
    return k_ref[0, pl.ds(pl.multiple_of(j * K_BLOCK, K_BLOCK), K_BLOCK), :]


def _head_values(vT_ref, blocks, h):
    rows = pl.ds(h * HEAD_DIM, HEAD_DIM)
    return jnp.concatenate([vT_ref[0, j, rows, :] for j in blocks], axis=1)


def _sb_decay(zT, ut, mask):
    spf = jnp.maximum(zT, jnp.log(1.0 + jnp.exp2(jnp.minimum(zT, EXP2_CLAMP))) * LOG2E)
    if mask is not None:
        spf = jnp.where(mask, spf, 0.0)
    cum = _dot(ut, spf.astype(BF16))
    return zT - spf, cum, cum[0:1, :] + spf[0:1, :]


def _scores(qT_h, h, k_blks):
    pair = slice((h // 2) * LANES, (h // 2 + 1) * LANES)
    return [_dot(kb[:, pair], qT_h) for kb in k_blks]


def _sb_weights(parts, vT_h, carry, masks, valid):
    weights = []
    for (log_beta, cum, total), mask, ok in zip(parts, masks, valid):
        shift = carry if ok is None else jnp.where(ok, carry, -MASKED_SCORE)
        a = jnp.exp2(log_beta - cum - shift)
        if mask is not None:
            a = jnp.where(mask, a, 0.0)
        weights.append(a.astype(BF16))
        carry = carry + (total if ok is None else jnp.where(ok, total, 0.0))
    return _dot(vT_h, jnp.concatenate(weights, axis=0)), carry


def _ca_head(raw, bias_ref, h, first, vT_ref, blocks):
    kb = K_BLOCK
    half = Q_BLOCK // 2
    n_live = CA_WINDOW_BLOCKS - 1
    out = []
    for side in range(2):
        cols = slice(side * half, (side + 1) * half)
        scores = []
        for m in range(side, side + n_live):
            s = raw[m][:, cols] + bias_ref[h, m * kb:(m + 1) * kb, cols]
            if m < CA_WINDOW_BLOCKS - Q_BLOCK // kb:
                s = jnp.where(first + m >= 0, s, MASKED_SCORE)
            scores.append(s)
        sT = jnp.concatenate(scores, axis=0)
        p = jnp.exp2(sT - jnp.max(sT, axis=0, keepdims=True))
        den = jnp.sum(p, axis=0, keepdims=True)
        out.append(_dot(_head_values(vT_ref, blocks[side:side + n_live], h), p.astype(BF16)) / den)
    return jnp.concatenate(out, axis=1)


def _attn_kernel(qsb_ref, ksb_ref, vsb_ref, qca_ref, kca_ref, vca_ref, ut_ref, bias_ref,
                 osb_ref, oca_ref, acc_ref):
    i = pl.program_id(2)
    qb, kb = Q_BLOCK, K_BLOCK
    per_q = qb // kb
    heads = range(STEP_HEADS)
    ut = ut_ref[...]
    key_pos = lax.broadcasted_iota(jnp.int32, (kb, qb), 0)
    qry_pos = lax.broadcasted_iota(jnp.int32, (kb, qb), 1)

    sb_q = _head_queries(qsb_ref[0])
    newest = i * per_q + per_q - 1
    n_back = SB_FIRST_BLOCKS - per_q
    has_back = i * per_q >= n_back
    sb_blocks = [jnp.maximum(newest - g, 0) for g in range(SB_FIRST_BLOCKS)]
    masks = [(key_pos + (per_q - 1 - g) * kb) < qry_pos for g in range(per_q)] + [None] * n_back
    valid = [None] * per_q + [has_back] * n_back
    ca_q = _head_queries(qca_ref[0])
    first = i * per_q - (CA_WINDOW_BLOCKS - per_q)
    ca_blocks = [jnp.maximum(first + m, 0) for m in range(CA_WINDOW_BLOCKS)]

    sb_k = [_key_block(ksb_ref, j) for j in sb_blocks]
    ca_k = [_key_block(kca_ref, j) for j in ca_blocks]
    sb_scores = [_scores(sb_q[h], h, sb_k) for h in heads]
    sb_parts = [[_sb_decay(z, ut, m) for z, m in zip(sb_scores[h], masks)] for h in heads]
    carries = []
    for h in heads:
        pv, carry = _sb_weights(sb_parts[h], _head_values(vsb_ref, sb_blocks, h),
                                jnp.zeros((1, qb), F32), masks, valid)
        acc_ref[h] = pv
        carries.append(carry)
    ca_scores = [_scores(ca_q[h], h, ca_k) for h in heads]
    ca_out = [_ca_head(ca_scores[h], bias_ref, h, first, vca_ref, ca_blocks) for h in heads]
    oca_ref[0] = jnp.concatenate(ca_out, axis=0).T.astype(BF16)

    def alive(state):
        j, carries = state
        least = functools.reduce(jnp.minimum, carries)
        return (j >= 0) & (jnp.min(least) < SB_DEAD_LOG2)

    def tail(state):
        j, carries = state
        blocks = [j - g for g in range(SB_TAIL_BLOCKS)]
        k_blks = [_key_block(ksb_ref, jb) for jb in blocks]
        no_mask = [None] * SB_TAIL_BLOCKS
        scores = [_scores(sb_q[h], h, k_blks) for h in heads]
        parts = [[_sb_decay(z, ut, None) for z in scores[h]] for h in heads]
        out = []
        for h in heads:
            pv, carry = _sb_weights(parts[h], _head_values(vsb_ref, blocks, h), carries[h], no_mask, no_mask)
            acc_ref[h] += pv
            out.append(carry)
        return j - SB_TAIL_BLOCKS, tuple(out)

    lax.while_loop(alive, tail, (newest - SB_FIRST_BLOCKS, tuple(carries)))

    oT = jnp.concatenate([acc_ref[h] for h in heads], axis=0)
    osb_ref[0] = oT.T.astype(BF16)


def _ca_bias_kernel(ext_ref, o_ref):
    n_keys, span = o_ref.shape[1], ext_ref.shape[2]
    rows = jnp.broadcast_to(ext_ref[0], (n_keys, span))
    table = pltpu.roll(rows, span - (n_keys - 1), axis=1, stride=1, stride_axis=0)[:, :Q_BLOCK]
    key_chunk = lax.broadcasted_iota(jnp.int32, (n_keys, Q_BLOCK), 0) // CHUNK
    qry_chunk = lax.broadcasted_iota(jnp.int32, (n_keys, Q_BLOCK), 1) // CHUNK
    band = (key_chunk >= qry_chunk) & (key_chunk <= qry_chunk + CA_PREV_CHUNKS)
    o_ref[0] = jnp.where(band, table * LOG2E, MASKED_SCORE)


def _ca_bias_table(rel_bias):
    n_keys = CA_WINDOW_BLOCKS * K_BLOCK
    heads = rel_bias.shape[0]
    lo_dist = (n_keys - Q_BLOCK) - (n_keys - 1)
    assert lo_dist >= -REL_CLIP
    span = pl.next_power_of_2(n_keys + Q_BLOCK - 1)
    head_part = rel_bias[:, REL_CLIP + lo_dist:].astype(F32)
    tail = jnp.broadcast_to(rel_bias[:, -1:].astype(F32), (heads, span - head_part.shape[1]))
    ext = jnp.concatenate([head_part, tail], axis=1).reshape(heads, 1, span)
    return pl.pallas_call(
        _ca_bias_kernel,
        grid=(heads,),
        in_specs=[pl.BlockSpec((1, 1, span), lambda h: (h, 0, 0))],
        out_specs=pl.BlockSpec((1, n_keys, Q_BLOCK), lambda h: (h, 0, 0)),
        out_shape=jax.ShapeDtypeStruct((heads, n_keys, Q_BLOCK), F32),
        compiler_params=_params(1),
        name="ca_bias_table",
    )(ext)


def _attention(sb_qkv, ca_qkv, bias_table):
    batch, _, seq = sb_qkv[0].shape
    qb, kb = Q_BLOCK, K_BLOCK
    n_keys = CA_WINDOW_BLOCKS * kb
    assert n_keys - qb == CA_PREV_CHUNKS * CHUNK and qb == 2 * kb and kb == 2 * CHUNK
    assert (qb // kb) % SB_TAIL_BLOCKS == 0 and (SB_FIRST_BLOCKS - qb // kb) % SB_TAIL_BLOCKS == 0
    assert seq % qb == 0 and HEADS % STEP_HEADS == 0 and STEP_HEADS % 2 == 0
    row = np.arange(kb)[:, None]
    col = np.arange(kb)[None, :]
    ut = jnp.asarray((col > row).astype(np.float32), BF16)
    feat = STEP_HEADS * HEAD_DIM
    qkv_specs = [pl.BlockSpec((1, feat, qb), lambda b, p, i: (b, p, i)),
                 pl.BlockSpec((1, seq, feat), lambda b, p, i: (b, 0, p)),
                 pl.BlockSpec((1, seq // LANES, feat, LANES), lambda b, p, i: (b, 0, p, 0))]
    out_spec = pl.BlockSpec((1, qb, feat), lambda b, p, i: (b, i, p))
    out_shape = jax.ShapeDtypeStruct((batch, seq, WIDTH), BF16)
    return pl.pallas_call(
        _attn_kernel,
        grid=(batch, HEADS // STEP_HEADS, seq // qb),
        in_specs=qkv_specs + qkv_specs + [
            pl.BlockSpec((kb, kb), lambda b, p, i: (0, 0)),
            pl.BlockSpec((STEP_HEADS, n_keys, qb), lambda b, p, i: (p, 0, 0))],
        out_specs=[out_spec, out_spec],
        out_shape=[out_shape, out_shape],
        scratch_shapes=[pltpu.VMEM((STEP_HEADS, HEAD_DIM, qb), F32)],
        compiler_params=_params(3),
        name="attention",
    )(*sb_qkv, *ca_qkv, ut, bias_table)


def _layer_norm(y, g, b):
    mu = jnp.mean(y, axis=-1, keepdims=True)
    d = y - mu
    var = jnp.mean(d * d, axis=-1, keepdims=True)
    return d * lax.rsqrt(var + LN_EPS) * g + b


def _mix_kernel(x_ref, osb_ref, oca_ref, wg_ref, bg_ref, wsb_ref, wca_ref, wout_ref,
                g_ref, b_ref, o_ref):
    x = x_ref[...]
    d_model = x.shape[-1]
    logits = _dot(x.astype(BF16), wg_ref[...]) + bg_ref[...]
    gates = jax.nn.sigmoid(logits)
    y_sb = _dot(osb_ref[...], wsb_ref[...])
    y_ca = _dot(oca_ref[...], wca_ref[...])
    merged = gates[:, :d_model] * y_sb + gates[:, d_model:] * y_ca
    y = ALPHA * x + _dot(merged.astype(BF16), wout_ref[...])
    o_ref[...] = _layer_norm(y, g_ref[...], b_ref[...])


def _mix(x2d, o_sb, o_ca, wg, bg, wsb, wca, wout, g, b):
    tokens, d_model = x2d.shape
    tm = TOKEN_TILE
    const = lambda a: pl.BlockSpec(a.shape, lambda t: (0,) * a.ndim)
    return pl.pallas_call(
        _mix_kernel,
        grid=(tokens // tm,),
        in_specs=[pl.BlockSpec((tm, d_model), lambda t: (t, 0)),
                  pl.BlockSpec((tm, WIDTH), lambda t: (t, 0)),
                  pl.BlockSpec((tm, WIDTH), lambda t: (t, 0)),
                  const(wg), const(bg), const(wsb), const(wca), const(wout), const(g), const(b)],
        out_specs=pl.BlockSpec((tm, d_model), lambda t: (t, 0)),
        out_shape=jax.ShapeDtypeStruct((tokens, d_model), F32),
        compiler_params=_params(1),
        name="mix_ln1",
    )(x2d, o_sb, o_ca, wg, bg, wsb, wca, wout, g, b)


def _mlp_kernel(x_ref, w1_ref, w2_ref, g_ref, b_ref, o_ref, *, ff_chunk):
    x = x_ref[...]
    xb = x.astype(BF16)
    d_ff = w1_ref.shape[1]
    acc = ALPHA * x
    for c in range(d_ff // ff_chunk):
        cols = slice(c * ff_chunk, (c + 1) * ff_chunk)
        h = jnp.maximum(_dot(xb, w1_ref[:, cols]), 0.0)
        acc += _dot((h * h).astype(BF16), w2_ref[cols, :])
    o_ref[...] = _layer_norm(acc, g_ref[...], b_ref[...])


def _mlp(x2d, w1, w2, g, b):
    tokens, d_model = x2d.shape
    tm = TOKEN_TILE
    const = lambda a: pl.BlockSpec(a.shape, lambda t: (0,) * a.ndim)
    return pl.pallas_call(
        functools.partial(_mlp_kernel, ff_chunk=1024),
        grid=(tokens // tm,),
        in_specs=[pl.BlockSpec((tm, d_model), lambda t: (t, 0)),
                  const(w1), const(w2), const(g), const(b)],
        out_specs=pl.BlockSpec((tm, d_model), lambda t: (t, 0)),
        out_shape=jax.ShapeDtypeStruct((tokens, d_model), F32),
        compiler_params=_params(1),
        name="mlp_ln2",
    )(x2d, w1, w2, g, b)


def kernel(x, w_in, b_gate, w_sb_proj, w_ca_proj, rel_bias, w_out, ln1_g, ln1_b,
           w_mlp_in, w_mlp_out, ln2_g, ln2_b):
    batch, seq, d_model = x.shape
    assert seq % TOKEN_TILE == 0 and seq % Q_BLOCK == 0 and d_model % LANES == 0
    x2d = x.reshape(batch * seq, d_model)

    w = WIDTH
    col = lambda n: w_in[:, n * w:(n + 1) * w]
    w_normal = jnp.concatenate([col(1), col(4)], axis=1).astype(BF16)
    w_transposed = jnp.concatenate([col(0), col(3), col(2), col(5)], axis=1).T.astype(BF16)
    w_gate = w_in[:, 6 * w:].astype(BF16)

    k_sb, k_ca, q_sbT, q_caT, v_sbT, v_caT = _in_projection(x2d, w_normal, w_transposed, batch, seq)

    o_sb, o_ca = _attention((q_sbT, k_sb.reshape(batch, seq, w), v_sbT),
                            (q_caT, k_ca.reshape(batch, seq, w), v_caT), _ca_bias_table(rel_bias))

    row = lambda a: a.reshape(1, -1).astype(F32)
    x1 = _mix(x2d, o_sb.reshape(batch * seq, w), o_ca.reshape(batch * seq, w),
              w_gate, row(b_gate), w_sb_proj.astype(BF16), w_ca_proj.astype(BF16),
              w_out.astype(BF16), row(ln1_g), row(ln1_b))
    x2 = _mlp(x1, w_mlp_in.astype(BF16), w_mlp_out.astype(BF16), row(ln2_g), row(ln2_b))
    return x2.reshape(batch, seq, d_model)
```

```python
import functools

import jax
import jax.numpy as jnp
import numpy as np
from jax import lax
from jax.experimental import pallas as pl
from jax.experimental.pallas import tpu as pltpu

F32 = jnp.float32
BF16 = jnp.bfloat16

HEADS = 8
HEAD_DIM = 64
WIDTH = HEADS * HEAD_DIM
STEP_HEADS = 8
CHUNK = 64
CA_PREV_CHUNKS = 8
REL_CLIP = 256
DEPTH = 1
ALPHA = (2.0 * DEPTH) ** 0.25
LN_EPS = 1e-5
LOG2E = 1.4426950408889634
QK_SCALE = HEAD_DIM ** -0.5
Q_SB_COLS, K_SB_COLS, V_SB_COLS, Q_CA_COLS, K_CA_COLS, V_CA_COLS = range(6)
GATE_COLS = 3

LANES = 128
VMEM_LIMIT_BYTES = 56 * 1024 * 1024

TOKEN_TILE = 1024
CHANNEL_TILE = 1024
ROW_SUBTILES = 4
FF_CHUNK = 1024
Q_BLOCK = 256
K_BLOCK = 128
SB_FIRST_BLOCKS = 4
SB_TAIL_BLOCKS = 2
CA_WINDOW_BLOCKS = 6
MASKED_SCORE = -1e30
SB_DEAD_LOG2 = 150.0
EXP2_CLAMP = 126.0

_NT = (((1,), (1,)), ((), ()))


def _dot(a, b):
    return jnp.dot(a, b, preferred_element_type=F32)


def _params(n_axes, flags=None):
    return pltpu.CompilerParams(
        dimension_semantics=("arbitrary",) * n_axes,
        vmem_limit_bytes=VMEM_LIMIT_BYTES,
        flags=flags)


def _inproj_kernel(x_ref, wksb_ref, wkca_ref, wt_ref, *refs):
    n_later = (len(refs) - 6) // 2
    later_f32, (ksb_ref, kca_ref, qsbT_ref, qcaT_ref, vsbT_ref, vcaT_ref) = refs[:n_later], refs[n_later:n_later + 6]
    for src, dst in zip(later_f32, refs[n_later + 6:]):
        dst[...] = src[...].astype(BF16)
    xb = x_ref[...].astype(BF16)
    ksb_ref[...] = _dot(xb, wksb_ref[...]).astype(BF16)
    kca_ref[...] = _dot(xb, wkca_ref[...]).astype(BF16)
    t = lax.dot_general(wt_ref[...], xb, _NT, preferred_element_type=F32)
    w = WIDTH
    qsbT_ref[0] = (t[:w] * (QK_SCALE * LOG2E)).astype(BF16)
    qcaT_ref[0] = (t[w:2 * w] * (QK_SCALE * LOG2E)).astype(BF16)
    for c in range(TOKEN_TILE // LANES):
        cols = slice(c * LANES, (c + 1) * LANES)
        vsbT_ref[0, c] = t[2 * w:3 * w, cols].astype(BF16)
        vcaT_ref[0, c] = t[3 * w:, cols].astype(BF16)


def _in_projection(x2d, w_all, w_transposed, later_weights, batch, seq):
    tokens, d_model = x2d.shape
    n_steps = tokens // TOKEN_TILE
    later_specs = [pl.BlockSpec((a.shape[0] // n_steps, a.shape[1]), lambda t: (t, 0)) for a in later_weights]
    assert all(a.shape[0] % (16 * n_steps) == 0 for a in later_weights)
    resident = pl.Buffered(1)
    tm = TOKEN_TILE
    tiles_per_seq = seq // tm
    blocks_per_tile = tm // LANES
    row_spec = pl.BlockSpec((tm, WIDTH), lambda t: (t, 0))
    tr_spec = pl.BlockSpec((1, WIDTH, tm), lambda t: (t // tiles_per_seq, 0, t % tiles_per_seq))
    blk_spec = pl.BlockSpec((1, blocks_per_tile, WIDTH, LANES),
                            lambda t: (t // tiles_per_seq, t % tiles_per_seq, 0, 0))
    row_shape = jax.ShapeDtypeStruct((tokens, WIDTH), BF16)
    tr_shape = jax.ShapeDtypeStruct((batch, WIDTH, seq), BF16)
    blk_shape = jax.ShapeDtypeStruct((batch, seq // LANES, WIDTH, LANES), BF16)
    return pl.pallas_call(
        _inproj_kernel,
        grid=(tokens // tm,),
        in_specs=[pl.BlockSpec((tm, d_model), lambda t: (t, 0)),
                  pl.BlockSpec((d_model, WIDTH), lambda t: (0, K_SB_COLS), pipeline_mode=resident),
                  pl.BlockSpec((d_model, WIDTH), lambda t: (0, K_CA_COLS), pipeline_mode=resident),
                  pl.BlockSpec(w_transposed.shape, lambda t: (0, 0), pipeline_mode=resident)] + later_specs,
        out_specs=[row_spec, row_spec, tr_spec, tr_spec, blk_spec, blk_spec] + later_specs,
        out_shape=[row_shape, row_shape, tr_shape, tr_shape, blk_shape, blk_shape]
        + [jax.ShapeDtypeStruct(a.shape, BF16) for a in later_weights],
        compiler_params=_params(1),
        name="in_projection",
    )(x2d, w_all, w_all, w_transposed, *later_weights)


def _head_queries(qT):
    zeros = jnp.zeros((HEAD_DIM, qT.shape[1]), qT.dtype)
    out = []
    for h in range(STEP_HEADS):
        rows = qT[h * HEAD_DIM:(h + 1) * HEAD_DIM]
        out.append(jnp.concatenate([rows, zeros] if h % 2 == 0 else [zeros, rows], axis=0))
    return out


def _key_block(k_ref, j):
    return k_ref[0, pl.ds(pl.multiple_of(j * K_BLOCK, K_BLOCK), K_BLOCK), :]


def _head_values(vT_ref, blocks, h):
    rows = pl.ds(h * HEAD_DIM, HEAD_DIM)
    return jnp.concatenate([vT_ref[0, j, rows, :] for j in blocks], axis=1)


def _sb_decay(zT, ut, mask):
    spf = jnp.maximum(zT, jnp.log(1.0 + jnp.exp2(jnp.minimum(zT, EXP2_CLAMP))) * LOG2E)
    if mask is not None:
        spf = jnp.where(mask, spf, 0.0)
    cum = _dot(ut, spf.astype(BF16))
    return zT - spf, cum, cum[0:1, :] + spf[0:1, :]


def _scores(qT_h, h, k_blks, query_cols=None):
    pair = slice((h // 2) * LANES, (h // 2 + 1) * LANES)
    query_cols = query_cols or [slice(None)] * len(k_blks)
    return [_dot(kb[:, pair], qT_h[:, cols]) for kb, cols in zip(k_blks, query_cols)]


def _sb_weights(parts, vT_h, carry, masks, valid):
    weights = []
    for (log_beta, cum, total), mask, ok in zip(parts, masks, valid):
        q0 = carry.shape[1] - log_beta.shape[1]
        shift = carry if ok is None else jnp.where(ok, carry, -MASKED_SCORE)
        a = jnp.exp2(log_beta - cum - shift[:, q0:])
        if mask is not None:
            a = jnp.where(mask, a, 0.0)
        a = a.astype(BF16)
        if q0:
            a = jnp.concatenate([jnp.zeros((a.shape[0], q0), BF16), a], axis=1)
            total = jnp.concatenate([jnp.zeros((1, q0), F32), total], axis=1)
        weights.append(a)
        carry = carry + (total if ok is None else jnp.where(ok, total, 0.0))
    return _dot(vT_h, jnp.concatenate(weights, axis=0)), carry


def _ca_head(raw, bias_ref, h, first, vT_ref, blocks):
    kb = K_BLOCK
    half = Q_BLOCK // 2
    n_live = CA_WINDOW_BLOCKS - 1
    out = []
    for side in range(2):
        cols = slice(side * half, (side + 1) * half)
        scores = []
        for m in range(side, side + n_live):
            s = raw[m][:, cols] + bias_ref[h, m * kb:(m + 1) * kb, cols]
            if m < CA_WINDOW_BLOCKS - Q_BLOCK // kb:
                s = jnp.where(first + m >= 0, s, MASKED_SCORE)
            scores.append(s)
        sT = jnp.concatenate(scores, axis=0)
        p = jnp.exp2(sT - jnp.max(sT, axis=0, keepdims=True))
        den = jnp.sum(p, axis=0, keepdims=True)
        out.append(_dot(_head_values(vT_ref, blocks[side:side + n_live], h), p.astype(BF16)) / den)
    return jnp.concatenate(out, axis=1)


def _attn_kernel(qsb_ref, ksb_ref, vsb_ref, qca_ref, kca_ref, vca_ref, ut_ref, bias_ref,
                 osb_ref, oca_ref, acc_ref):
    i = pl.program_id(2)
    qb, kb = Q_BLOCK, K_BLOCK
    per_q = qb // kb
    heads = range(STEP_HEADS)
    ut = ut_ref[...]
    key_pos = lax.broadcasted_iota(jnp.int32, (kb, qb), 0)
    qry_pos = lax.broadcasted_iota(jnp.int32, (kb, qb), 1)

    sb_q = _head_queries(qsb_ref[0])
    newest = i * per_q + per_q - 1
    n_back = SB_FIRST_BLOCKS - per_q
    has_back = i * per_q >= n_back
    sb_blocks = [jnp.maximum(newest - g, 0) for g in range(SB_FIRST_BLOCKS)]
    diag_q0 = [(per_q - 1 - g) * kb for g in range(per_q)]
    masks = [(key_pos + q0 < qry_pos)[:, q0:] for q0 in diag_q0] + [None] * n_back
    sb_cols = [slice(q0, None) for q0 in diag_q0] + [slice(None)] * n_back
    valid = [None] * per_q + [has_back] * n_back
    ca_q = _head_queries(qca_ref[0])
    first = i * per_q - (CA_WINDOW_BLOCKS - per_q)
    ca_blocks = [jnp.maximum(first + m, 0) for m in range(CA_WINDOW_BLOCKS)]

    sb_k = [_key_block(ksb_ref, j) for j in sb_blocks]
    ca_k = [_key_block(kca_ref, j) for j in ca_blocks]

    sb_scores = [_scores(sb_q[h], h, sb_k, sb_cols) for h in heads]
    sb_parts = [[_sb_decay(z, ut, m) for z, m in zip(sb_scores[h], masks)] for h in heads]
    carries, pvs = [], []
    for h in heads:
        pv, carry = _sb_weights(sb_parts[h], _head_values(vsb_ref, sb_blocks, h),
                                jnp.zeros((1, qb), F32), masks, valid)
        acc_ref[h] = pv
        pvs.append(pv)
        carries.append(carry)
    osb_ref[0] = jnp.concatenate(pvs, axis=0).T.astype(BF16)

    def any_live(carries):
        return jnp.min(functools.reduce(jnp.minimum, carries)) < SB_DEAD_LOG2

    live = any_live(carries)
    ca_scores = [_scores(ca_q[h], h, ca_k) for h in heads]
    ca_out = [_ca_head(ca_scores[h], bias_ref, h, first, vca_ref, ca_blocks) for h in heads]
    oca_ref[0] = jnp.concatenate(ca_out, axis=0).T.astype(BF16)

    def alive(state):
        j, live, _ = state
        return (j >= 0) & live

    def tail(state):
        j, _, carries = state
        blocks = [j - g for g in range(SB_TAIL_BLOCKS)]
        k_blks = [_key_block(ksb_ref, jb) for jb in blocks]
        no_mask = [None] * SB_TAIL_BLOCKS
        scores = [_scores(sb_q[h], h, k_blks) for h in heads]
        parts = [[_sb_decay(z, ut, None) for z in scores[h]] for h in heads]
        out = []
        for h in heads:
            pv, carry = _sb_weights(parts[h], _head_values(vsb_ref, blocks, h), carries[h], no_mask, no_mask)
            acc_ref[h] += pv
            out.append(carry)
        return j - SB_TAIL_BLOCKS, any_live(out), tuple(out)

    tail_start = newest - SB_FIRST_BLOCKS
    tail_end, _, _ = lax.while_loop(alive, tail, (tail_start, live, tuple(carries)))

    @pl.when(tail_end != tail_start)
    def _():
        oT = jnp.concatenate([acc_ref[h] for h in heads], axis=0)
        osb_ref[0] = oT.T.astype(BF16)


def _ca_bias_kernel(ext_ref, o_ref):
    n_keys, span = o_ref.shape[1], ext_ref.shape[2]
    rows = jnp.broadcast_to(ext_ref[0], (n_keys, span))
    table = pltpu.roll(rows, span - (n_keys - 1), axis=1, stride=1, stride_axis=0)[:, :Q_BLOCK]
    key_chunk = lax.broadcasted_iota(jnp.int32, (n_keys, Q_BLOCK), 0) // CHUNK
    qry_chunk = lax.broadcasted_iota(jnp.int32, (n_keys, Q_BLOCK), 1) // CHUNK
    band = (key_chunk >= qry_chunk) & (key_chunk <= qry_chunk + CA_PREV_CHUNKS)
    o_ref[0] = jnp.where(band, table * LOG2E, MASKED_SCORE)


def _ca_bias_table(rel_bias):
    n_keys = CA_WINDOW_BLOCKS * K_BLOCK
    heads = rel_bias.shape[0]
    lo_dist = (n_keys - Q_BLOCK) - (n_keys - 1)
    assert lo_dist >= -REL_CLIP
    span = pl.next_power_of_2(n_keys + Q_BLOCK - 1)
    head_part = rel_bias[:, REL_CLIP + lo_dist:].astype(F32)
    tail = jnp.broadcast_to(rel_bias[:, -1:].astype(F32), (heads, span - head_part.shape[1]))
    ext = jnp.concatenate([head_part, tail], axis=1).reshape(heads, 1, span)
    return pl.pallas_call(
        _ca_bias_kernel,
        grid=(heads,),
        in_specs=[pl.BlockSpec((1, 1, span), lambda h: (h, 0, 0))],
        out_specs=pl.BlockSpec((1, n_keys, Q_BLOCK), lambda h: (h, 0, 0)),
        out_shape=jax.ShapeDtypeStruct((heads, n_keys, Q_BLOCK), F32),
        compiler_params=_params(1),
        name="ca_bias_table",
    )(ext)


def _attention(sb_qkv, ca_qkv, bias_table):
    batch, _, seq = sb_qkv[0].shape
    qb, kb = Q_BLOCK, K_BLOCK
    n_keys = CA_WINDOW_BLOCKS * kb
    assert n_keys - qb == CA_PREV_CHUNKS * CHUNK and qb == 2 * kb and kb == 2 * CHUNK
    assert (qb // kb) % SB_TAIL_BLOCKS == 0 and (SB_FIRST_BLOCKS - qb // kb) % SB_TAIL_BLOCKS == 0
    assert seq % qb == 0 and HEADS % STEP_HEADS == 0 and STEP_HEADS % 2 == 0
    row = np.arange(kb)[:, None]
    col = np.arange(kb)[None, :]
    ut = jnp.asarray((col > row).astype(np.float32), BF16)
    feat = STEP_HEADS * HEAD_DIM
    qkv_specs = [pl.BlockSpec((1, feat, qb), lambda b, p, i: (b, p, i)),
                 pl.BlockSpec((1, seq, feat), lambda b, p, i: (b, 0, p)),
                 pl.BlockSpec((1, seq // LANES, feat, LANES), lambda b, p, i: (b, 0, p, 0))]
    out_spec = pl.BlockSpec((1, qb, feat), lambda b, p, i: (b, i, p))
    out_shape = jax.ShapeDtypeStruct((batch, seq, WIDTH), BF16)
    return pl.pallas_call(
        _attn_kernel,
        grid=(batch, HEADS // STEP_HEADS, seq // qb),
        in_specs=qkv_specs + qkv_specs + [
            pl.BlockSpec((kb, kb), lambda b, p, i: (0, 0)),
            pl.BlockSpec((STEP_HEADS, n_keys, qb), lambda b, p, i: (p, 0, 0),
                         pipeline_mode=pl.Buffered(1 if STEP_HEADS == HEADS else 2))],
        out_specs=[out_spec, out_spec],
        out_shape=[out_shape, out_shape],
        scratch_shapes=[pltpu.VMEM((STEP_HEADS, HEAD_DIM, qb), F32)],
        compiler_params=_params(3),
        name="attention",
    )(*sb_qkv, *ca_qkv, ut, bias_table)


def _row_subtiles(n_rows):
    size = n_rows // ROW_SUBTILES
    return [slice(r * size, (r + 1) * size) for r in range(ROW_SUBTILES)]


def _layer_norm(y, g, b):
    mu = jnp.mean(y, axis=-1, keepdims=True)
    d = y - mu
    var = jnp.mean(d * d, axis=-1, keepdims=True)
    return d * lax.rsqrt(var + LN_EPS) * g + b


def _mix_kernel(x_ref, osb_ref, oca_ref, wgsb_ref, wgca_ref, bg_ref, wsb_ref, wca_ref, wout_ref,
                g_ref, b_ref, o_ref):
    d_model = x_ref.shape[-1]
    for rows in _row_subtiles(x_ref.shape[0]):
        x = x_ref[rows, :]
        xb = x.astype(BF16)
        gate_sb = jax.nn.sigmoid(_dot(xb, wgsb_ref[...]) + bg_ref[:, :d_model])
        gate_ca = jax.nn.sigmoid(_dot(xb, wgca_ref[...]) + bg_ref[:, d_model:])
        y_sb = _dot(osb_ref[rows, :], wsb_ref[...])
        y_ca = _dot(oca_ref[rows, :], wca_ref[...])
        merged = gate_sb * y_sb + gate_ca * y_ca
        y = ALPHA * x + _dot(merged.astype(BF16), wout_ref[...])
        o_ref[rows, :] = _layer_norm(y, g_ref[...], b_ref[...])


def _mix(x2d, o_sb, o_ca, w_all, bg, wsb, wca, wout, g, b):
    tokens, d_model = x2d.shape
    tm = CHANNEL_TILE
    resident = pl.Buffered(1)
    const = lambda a: pl.BlockSpec(a.shape, lambda t: (0,) * a.ndim, pipeline_mode=resident)
    gate_cols = lambda n: pl.BlockSpec((d_model, d_model), lambda t: (0, GATE_COLS + n),
                                       pipeline_mode=resident)
    return pl.pallas_call(
        _mix_kernel,
        grid=(tokens // tm,),
        in_specs=[pl.BlockSpec((tm, d_model), lambda t: (t, 0)),
                  pl.BlockSpec((tm, WIDTH), lambda t: (t, 0)),
                  pl.BlockSpec((tm, WIDTH), lambda t: (t, 0)),
                  gate_cols(0), gate_cols(1),
                  const(bg), const(wsb), const(wca), const(wout), const(g), const(b)],
        out_specs=pl.BlockSpec((tm, d_model), lambda t: (t, 0)),
        out_shape=jax.ShapeDtypeStruct((tokens, d_model), F32),
        compiler_params=_params(1),
        name="mix_ln1",
    )(x2d, o_sb, o_ca, w_all, w_all, bg, wsb, wca, wout, g, b)


def _mlp_kernel(x_ref, w1_ref, w2_ref, g_ref, b_ref, o_ref):
    d_ff = w1_ref.shape[1]
    for rows in _row_subtiles(x_ref.shape[0]):
        x = x_ref[rows, :]
        xb = x.astype(BF16)
        acc = ALPHA * x
        for c in range(d_ff // FF_CHUNK):
            cols = slice(c * FF_CHUNK, (c + 1) * FF_CHUNK)
            h = jnp.maximum(_dot(xb, w1_ref[:, cols]), 0.0)
            acc += _dot((h * h).astype(BF16), w2_ref[cols, :])
        o_ref[rows, :] = _layer_norm(acc, g_ref[...], b_ref[...])


def _mlp(x2d, w1, w2, g, b):
    tokens, d_model = x2d.shape
    tm = CHANNEL_TILE
    const = lambda a: pl.BlockSpec(a.shape, lambda t: (0,) * a.ndim, pipeline_mode=pl.Buffered(1))
    return pl.pallas_call(
        _mlp_kernel,
        grid=(tokens // tm,),
        in_specs=[pl.BlockSpec((tm, d_model), lambda t: (t, 0)),
                  const(w1), const(w2), const(g), const(b)],
        out_specs=pl.BlockSpec((tm, d_model), lambda t: (t, 0)),
        out_shape=jax.ShapeDtypeStruct((tokens, d_model), F32),
        compiler_params=_params(1),
        name="mlp_ln2",
    )(x2d, w1, w2, g, b)


def kernel(x, w_in, b_gate, w_sb_proj, w_ca_proj, rel_bias, w_out, ln1_g, ln1_b,
           w_mlp_in, w_mlp_out, ln2_g, ln2_b):
    batch, seq, d_model = x.shape
    assert seq % TOKEN_TILE == 0 and seq % Q_BLOCK == 0 and d_model % LANES == 0
    assert (batch * seq) % CHANNEL_TILE == 0 and CHANNEL_TILE % (8 * ROW_SUBTILES) == 0
    x2d = x.reshape(batch * seq, d_model)

    w = WIDTH
    assert d_model == 2 * w and w_in.shape[1] == 6 * w + 2 * d_model
    w_all = w_in.astype(BF16)
    col = lambda n: w_all[:, n * w:(n + 1) * w]
    w_transposed = jnp.concatenate([col(Q_SB_COLS), col(Q_CA_COLS), col(V_SB_COLS), col(V_CA_COLS)], axis=1).T

    later_weights = (w_sb_proj, w_ca_proj, w_out, w_mlp_in, w_mlp_out)
    (k_sb, k_ca, q_sbT, q_caT, v_sbT, v_caT, *later_bf16) = _in_projection(
        x2d, w_all, w_transposed, later_weights, batch, seq)
    w_sb_b, w_ca_b, w_out_b, w_mlp_in_b, w_mlp_out_b = later_bf16

    o_sb, o_ca = _attention((q_sbT, k_sb.reshape(batch, seq, w), v_sbT),
                            (q_caT, k_ca.reshape(batch, seq, w), v_caT), _ca_bias_table(rel_bias))

    row = lambda a: a.reshape(1, -1).astype(F32)
    x1 = _mix(x2d, o_sb.reshape(batch * seq, w), o_ca.reshape(batch * seq, w),
              w_all, row(b_gate), w_sb_b, w_ca_b, w_out_b, row(ln1_g), row(ln1_b))
    x2 = _mlp(x1, w_mlp_in_b, w_mlp_out_b, row(ln2_g), row(ln2_b))
    return x2.reshape(batch, seq, d_model)
```
